```python
import jax, jax.numpy as jnp
from jax import lax
import numpy as np

D_MODEL = 1024
BATCH = 2
SEQ = 8192
DEPTH = 4
DEC_BATCH = 16
DEC_SEQ = 64
PAST_LEN = 2048

CHUNK = 64
N_MIXERS = 3
LAYER_MIXER = tuple(i % N_MIXERS for i in range(DEPTH))
EPS = 1e-6

D_RNN = D_MODEL
LRU_BLOCKS = 8
LRU_BW = D_RNN // LRU_BLOCKS
LRU_CONV = 4
LRU_C = 8.0
N_LRU_LAYERS = len([i for i in range(DEPTH) if i % N_MIXERS == 0])

D_INNER = 2 * D_MODEL
SSD_HEAD_DIM = 64
SSD_HEADS = D_INNER // SSD_HEAD_DIM
SSD_GROUPS = 4
SSD_HPG = SSD_HEADS // SSD_GROUPS
SSD_STATE = 128
SSD_CONV = 4
SSD_CONV_DIM = D_INNER + 2 * SSD_GROUPS * SSD_STATE
SSD_IN_DIM = D_INNER + SSD_CONV_DIM + SSD_HEADS
SSD_CHUNK = CHUNK

FOX_HEADS = 16
FOX_HEAD_DIM = D_MODEL // FOX_HEADS
FOX_QBLOCK = 128
FOX_SCALE = FOX_HEAD_DIM ** -0.5
FOX_BF_LO = 2.2
FOX_BF_HI = 6.9

D_FF = 2816
FFN_CONV = 3

kernel_name = 'hybrid_streaming_encoder_step'


def _rmsnorm(x, g):
    xf = x.astype(jnp.float32)
    y = xf * lax.rsqrt(jnp.mean(xf * xf, axis=-1, keepdims=True) + EPS)
    return (y * g.astype(jnp.float32)).astype(x.dtype)


def _ada(c, w, b):
    m = jax.nn.silu(c) @ w + b
    return jnp.split(m[:, None, :], 6, axis=-1)


def _causal_dwconv(x, prev, w, b):
    width = w.shape[0]
    T = x.shape[1]
    xx = jnp.concatenate([prev.astype(x.dtype), x], axis=1)
    y = b + w[0] * xx[:, 0:T]
    for k in range(1, width):
        y = y + w[k] * xx[:, k:k + T]
    return y, xx[:, T:]


def _linear_scan(a, b, h0):
    def comb(l, r):
        return l[0] * r[0], r[0] * l[1] + r[1]
    A, Bc = lax.associative_scan(comb, (a, b), axis=1)
    return A * h0.astype(jnp.float32)[:, None] + Bc


def _rglru_mixer(h, conv_prev, h_prev, w_x, b_x, w_y, b_y, conv_w, conv_b, w_a, b_a, w_i, b_i, lam, w_o, b_o):
    Bn, T, _ = h.shape
    gate = jax.nn.gelu(h @ w_y + b_y)
    u, conv_new = _causal_dwconv(h @ w_x + b_x, conv_prev, conv_w, conv_b)
    ub = u.reshape(Bn, T, LRU_BLOCKS, LRU_BW)
    r = jax.nn.sigmoid(jnp.einsum('btnk,nkj->btnj', ub, w_a).reshape(Bn, T, D_RNN) + b_a)
    i = jax.nn.sigmoid(jnp.einsum('btnk,nkj->btnj', ub, w_i).reshape(Bn, T, D_RNN) + b_i)
    log_a = -LRU_C * r.astype(jnp.float32) * jax.nn.softplus(-lam.astype(jnp.float32))
    a = jnp.exp(log_a)
    bterm = jnp.sqrt(-jnp.expm1(2.0 * log_a)) * (i * u).astype(jnp.float32)
    hs = _linear_scan(a, bterm, h_prev)
    y = (hs.astype(h.dtype) * gate) @ w_o + b_o
    return y, conv_new, hs[:, -1]


def _ssd_scan(x, dt, A, Bm, Cm, s0, L):
    b, T = x.shape[:2]
    nc = T // L
    x = x.reshape(b, nc, L, SSD_GROUPS, SSD_HPG, SSD_HEAD_DIM)
    dt = dt.reshape(b, nc, L, SSD_GROUPS, SSD_HPG)
    Bm = Bm.reshape(b, nc, L, SSD_GROUPS, SSD_STATE)
    Cm = Cm.reshape(b, nc, L, SSD_GROUPS, SSD_STATE)
    cum = jnp.cumsum(dt * A, axis=2)
    seg = cum[:, :, :, None] - cum[:, :, None, :]
    causal = jnp.tril(jnp.ones((L, L), bool))[:, :, None, None]
    decay = jnp.exp(jnp.where(causal, seg, -jnp.inf))
    xdt = x * dt[..., None]
    cb = jnp.einsum('bclgn,bcsgn->bclsg', Cm, Bm)
    y_diag = jnp.einsum('bclsg,bclsge,bcsgep->bclgep', cb, decay, xdt)
    decay_end = jnp.exp(cum[:, :, -1:] - cum)
    states = jnp.einsum('bclgn,bclge,bclgep->bcgepn', Bm, decay_end, xdt)
    chunk_decay = jnp.exp(cum[:, :, -1])

    def step(s, inp):
        dec, st = inp
        return s * dec[..., None, None] + st, s

    s_final, s_in = lax.scan(step, s0, (jnp.moveaxis(chunk_decay, 1, 0), jnp.moveaxis(states, 1, 0)))
    s_in = jnp.moveaxis(s_in, 0, 1)
    y_off = jnp.einsum('bclgn,bcgepn,bclge->bclgep', Cm, s_in, jnp.exp(cum))
    y = (y_diag + y_off).reshape(b, T, SSD_GROUPS, SSD_HPG, SSD_HEAD_DIM)
    return y, s_final


def _ssd_mixer(h, conv_prev, ssm_prev, w_in, conv_w, conv_b, dt_bias, a_log, d_skip, norm_g, w_out):
    Bn, T, _ = h.shape
    zxbcdt = h @ w_in
    z = zxbcdt[..., :D_INNER]
    xbc = zxbcdt[..., D_INNER:D_INNER + SSD_CONV_DIM]
    dt_raw = zxbcdt[..., D_INNER + SSD_CONV_DIM:]
    xbc, conv_new = _causal_dwconv(xbc, conv_prev, conv_w, conv_b)
    xbc = jax.nn.silu(xbc).astype(jnp.float32)
    gn = SSD_GROUPS * SSD_STATE
    xs = xbc[..., :D_INNER].reshape(Bn, T, SSD_GROUPS, SSD_HPG, SSD_HEAD_DIM)
    Bm = xbc[..., D_INNER:D_INNER + gn].reshape(Bn, T, SSD_GROUPS, SSD_STATE)
    Cm = xbc[..., D_INNER + gn:].reshape(Bn, T, SSD_GROUPS, SSD_STATE)
    dt = jax.nn.softplus(dt_raw.astype(jnp.float32) + dt_bias.astype(jnp.float32)).reshape(Bn, T, SSD_GROUPS, SSD_HPG)
    A = -jnp.exp(a_log.astype(jnp.float32)).reshape(SSD_GROUPS, SSD_HPG)
    s0 = ssm_prev.astype(jnp.float32).reshape(Bn, SSD_GROUPS, SSD_HPG, SSD_HEAD_DIM, SSD_STATE)
    L = SSD_CHUNK if T % SSD_CHUNK == 0 else T
    y, s_new = _ssd_scan(xs, dt, A, Bm, Cm, s0, L)
    y = y + d_skip.astype(jnp.float32).reshape(SSD_GROUPS, SSD_HPG)[:, :, None] * xs
    y = y.reshape(Bn, T, D_INNER)
    y = _rmsnorm(y * jax.nn.silu(z.astype(jnp.float32)), norm_g)
    return y.astype(h.dtype) @ w_out, conv_new, s_new.reshape(Bn, SSD_HEADS, SSD_HEAD_DIM, SSD_STATE)


def _fox_block(q_blk, fq_blk, qpos_blk, k_all, v_all, fk_t, kpos):
    s = jnp.einsum('bqhd,bkhd->bhqk', q_blk, k_all).astype(jnp.float32) * FOX_SCALE
    s = s + jnp.swapaxes(fq_blk, 1, 2)[..., :, None] - fk_t[..., None, :]
    s = jnp.where(kpos[None, :] <= qpos_blk[:, None], s, -jnp.inf)
    p = jax.nn.softmax(s, axis=-1)
    return jnp.einsum('bhqk,bkhd->bqhd', p.astype(v_all.dtype), v_all)


def _fox_mixer(h, k_past, v_past, lf_past, w_qkv, w_f, b_f, w_o):
    Bn, T, _ = h.shape
    qkv = (h @ w_qkv).reshape(Bn, T, 3, FOX_HEADS, FOX_HEAD_DIM)
    q, k, v = qkv[:, :, 0], qkv[:, :, 1], qkv[:, :, 2]
    lf = jax.nn.log_sigmoid((h @ w_f).astype(jnp.float32) + b_f.astype(jnp.float32))
    if k_past is None:
        k_all, v_all, lf_all = k, v, lf
    else:
        k_all = jnp.concatenate([k_past.astype(k.dtype), k], axis=1)
        v_all = jnp.concatenate([v_past.astype(v.dtype), v], axis=1)
        lf_all = jnp.concatenate([lf_past.astype(jnp.float32), lf], axis=1)
    P = k_all.shape[1] - T
    F = jnp.cumsum(lf_all, axis=1)
    fk_t = jnp.swapaxes(F, 1, 2)
    fq = F[:, P:]
    kpos = jnp.arange(P + T)
    qpos = P + jnp.arange(T)
    if T <= FOX_QBLOCK:
        o = _fox_block(q, fq, qpos, k_all, v_all, fk_t, kpos)
    else:
        nb = T // FOX_QBLOCK
        qs = jnp.swapaxes(q.reshape(Bn, nb, FOX_QBLOCK, FOX_HEADS, FOX_HEAD_DIM), 0, 1)
        fqs = jnp.swapaxes(fq.reshape(Bn, nb, FOX_QBLOCK, FOX_HEADS), 0, 1)
        qps = qpos.reshape(nb, FOX_QBLOCK)
        o = lax.map(lambda a: _fox_block(a[0], a[1], a[2], k_all, v_all, fk_t, kpos), (qs, fqs, qps))
        o = jnp.swapaxes(o, 0, 1)
    y = o.reshape(Bn, T, D_MODEL) @ w_o
    return y, k, v, lf


def _conv_ffn(h, conv_prev, w_gate, w_up, conv_w, conv_b, w_down):
    g, conv_new = _causal_dwconv(h @ w_gate, conv_prev, conv_w, conv_b)
    return (jax.nn.gelu(g) * (h @ w_up)) @ w_down, conv_new


def _run_trunk(x, c, prev, W):
    Bn = x.shape[0]
    fresh = prev is None
    new = {}
    ffn_bufs = []
    for layer in range(DEPTH):
        sh1, sc1, g1, sh2, sc2, g2 = _ada(c, W['ada_w'][layer], W['ada_b'][layer])
        ng = W['norm_g'][layer]
        h = _rmsnorm(x, ng[0]) * (1 + sc1) + sh1
        kind = LAYER_MIXER[layer]
        tag = 'l%d' % layer
        if kind == 0:
            j = layer // N_MIXERS
            conv_prev = jnp.zeros((Bn, LRU_CONV - 1, D_RNN), x.dtype) if fresh else prev[tag + '_conv']
            h_prev = jnp.zeros((Bn, D_RNN), jnp.float32) if fresh else prev[tag + '_h']
            y, cn, hn = _rglru_mixer(h, conv_prev, h_prev, W['lru_w_x'][j], W['lru_b_x'][j], W['lru_w_y'][j], W['lru_b_y'][j],
                                     W['lru_conv_w'][j], W['lru_conv_b'][j], W['lru_w_a'][j], W['lru_b_a'][j],
                                     W['lru_w_i'][j], W['lru_b_i'][j], W['lru_lambda'][j], W['lru_w_o'][j], W['lru_b_o'][j])
            new[tag + '_conv'] = cn
            new[tag + '_h'] = hn
        elif kind == 1:
            conv_prev = jnp.zeros((Bn, SSD_CONV - 1, SSD_CONV_DIM), x.dtype) if fresh else prev[tag + '_conv']
            s_prev = jnp.zeros((Bn, SSD_HEADS, SSD_HEAD_DIM, SSD_STATE), jnp.float32) if fresh else prev[tag + '_ssm']
            y, cn, sn = _ssd_mixer(h, conv_prev, s_prev, W['ssd_w_in'], W['ssd_conv_w'], W['ssd_conv_b'], W['ssd_dt_bias'],
                                   W['ssd_a_log'], W['ssd_d'], W['ssd_norm_g'], W['ssd_w_out'])
            new[tag + '_conv'] = cn
            new[tag + '_ssm'] = sn
        else:
            kp = None if fresh else prev[tag + '_k']
            vp = None if fresh else prev[tag + '_v']
            lp = None if fresh else prev[tag + '_logf']
            y, kn, vn, ln = _fox_mixer(h, kp, vp, lp, W['fox_w_qkv'], W['fox_w_f'], W['fox_b_f'], W['fox_w_o'])
            new[tag + '_k'] = kn
            new[tag + '_v'] = vn
            new[tag + '_logf'] = ln
        x = x + g1 * _rmsnorm(y, ng[1])
        h = _rmsnorm(x, ng[2]) * (1 + sc2) + sh2
        ffn_prev = jnp.zeros((Bn, FFN_CONV - 1, D_FF), x.dtype) if fresh else prev['ffn_conv'][layer]
        y, buf = _conv_ffn(h, ffn_prev, W['ffn_w_gate'][layer], W['ffn_w_up'][layer], W['ffn_conv_w'][layer],
                           W['ffn_conv_b'][layer], W['ffn_w_down'][layer])
        ffn_bufs.append(buf)
        x = x + g2 * _rmsnorm(y, ng[3])
    new['ffn_conv'] = jnp.stack(ffn_bufs)
    return x, new


def setup_inputs(seed: int = 0) -> dict:
    key = jax.random.key(seed)
    ks = iter(jax.random.split(key, 64))
    f32 = jnp.float32

    def nrm(shape, scale):
        return scale * jax.random.normal(next(ks), shape, f32)

    def uni(shape, lo, hi):
        return jax.random.uniform(next(ks), shape, f32, lo, hi)

    D = D_MODEL
    L = N_LRU_LAYERS
    dt0 = jnp.exp(uni((SSD_HEADS,), float(np.log(1e-3)), float(np.log(1e-1))))
    return {
        'x_prompt': nrm((BATCH, SEQ, D), 1.0),
        'x_sample': nrm((DEC_BATCH, DEC_SEQ, D), 1.0),
        'c_prompt': nrm((BATCH, D), 1.0),
        'c_sample': nrm((DEC_BATCH, D), 1.0),
        'state_l0_conv': nrm((DEC_BATCH, LRU_CONV - 1, D_RNN), 1.0),
        'state_l0_h': nrm((DEC_BATCH, D_RNN), 0.3),
        'state_l1_conv': nrm((DEC_BATCH, SSD_CONV - 1, SSD_CONV_DIM), 1.0),
        'state_l1_ssm': nrm((DEC_BATCH, SSD_HEADS, SSD_HEAD_DIM, SSD_STATE), 0.1),
        'cache_l2_k': nrm((DEC_BATCH, PAST_LEN, FOX_HEADS, FOX_HEAD_DIM), 1.0),
        'cache_l2_v': nrm((DEC_BATCH, PAST_LEN, FOX_HEADS, FOX_HEAD_DIM), 1.0),
        'cache_l2_logf': jax.nn.log_sigmoid(uni((DEC_BATCH, PAST_LEN, FOX_HEADS), FOX_BF_LO, FOX_BF_HI)),
        'state_l3_conv': nrm((DEC_BATCH, LRU_CONV - 1, D_RNN), 1.0),
        'state_l3_h': nrm((DEC_BATCH, D_RNN), 0.3),
        'state_ffn_conv': nrm((DEPTH, DEC_BATCH, FFN_CONV - 1, D_FF), 1.0),
        'ada_w': nrm((DEPTH, D, 6 * D), D ** -0.5),
        'ada_b': nrm((DEPTH, 6 * D), 0.02),
        'norm_g': 1.0 + nrm((DEPTH, 4, D), 0.02),
        'lru_w_x': nrm((L, D, D_RNN), D ** -0.5),
        'lru_b_x': nrm((L, D_RNN), 0.02),
        'lru_w_y': nrm((L, D, D_RNN), D ** -0.5),
        'lru_b_y': nrm((L, D_RNN), 0.02),
        'lru_conv_w': nrm((L, LRU_CONV, D_RNN), LRU_CONV ** -0.5),
        'lru_conv_b': nrm((L, D_RNN), 0.02),
        'lru_w_a': nrm((L, LRU_BLOCKS, LRU_BW, LRU_BW), LRU_BW ** -0.5),
        'lru_b_a': nrm((L, D_RNN), 0.02),
        'lru_w_i': nrm((L, LRU_BLOCKS, LRU_BW, LRU_BW), LRU_BW ** -0.5),
        'lru_b_i': nrm((L, D_RNN), 0.02),
        'lru_lambda': uni((L, D_RNN), 4.3, 9.0),
        'lru_w_o': nrm((L, D_RNN, D), D_RNN ** -0.5),
        'lru_b_o': nrm((L, D), 0.02),
        'ssd_w_in': nrm((D, SSD_IN_DIM), D ** -0.5),
        'ssd_conv_w': nrm((SSD_CONV, SSD_CONV_DIM), SSD_CONV ** -0.5),
        'ssd_conv_b': nrm((SSD_CONV_DIM,), 0.02),
        'ssd_dt_bias': dt0 + jnp.log(-jnp.expm1(-dt0)),
        'ssd_a_log': jnp.log(uni((SSD_HEADS,), 1.0, 16.0)),
        'ssd_d': 1.0 + nrm((SSD_HEADS,), 0.1),
        'ssd_norm_g': 1.0 + nrm((D_INNER,), 0.02),
        'ssd_w_out': nrm((D_INNER, D), D_INNER ** -0.5),
        'fox_w_qkv': nrm((D, 3 * D), D ** -0.5),
        'fox_w_f': nrm((D, FOX_HEADS), 0.1 * D ** -0.5),
        'fox_b_f': uni((FOX_HEADS,), FOX_BF_LO, FOX_BF_HI),
        'fox_w_o': nrm((D, D), D ** -0.5),
        'ffn_w_gate': nrm((DEPTH, D, D_FF), D ** -0.5),
        'ffn_w_up': nrm((DEPTH, D, D_FF), D ** -0.5),
        'ffn_conv_w': nrm((DEPTH, FFN_CONV, D_FF), FFN_CONV ** -0.5),
        'ffn_conv_b': nrm((DEPTH, D_FF), 0.02),
        'ffn_w_down': nrm((DEPTH, D_FF, D), D_FF ** -0.5),
    }


def reference(x_prompt, x_sample, c_prompt, c_sample,
              state_l0_conv, state_l0_h, state_l1_conv, state_l1_ssm,
              cache_l2_k, cache_l2_v, cache_l2_logf, state_l3_conv, state_l3_h, state_ffn_conv,
              ada_w, ada_b, norm_g,
              lru_w_x, lru_b_x, lru_w_y, lru_b_y, lru_conv_w, lru_conv_b, lru_w_a, lru_b_a,
              lru_w_i, lru_b_i, lru_lambda, lru_w_o, lru_b_o,
              ssd_w_in, ssd_conv_w, ssd_conv_b, ssd_dt_bias, ssd_a_log, ssd_d, ssd_norm_g, ssd_w_out,
              fox_w_qkv, fox_w_f, fox_b_f, fox_w_o,
              ffn_w_gate, ffn_w_up, ffn_conv_w, ffn_conv_b, ffn_w_down):
    W = dict(ada_w=ada_w, ada_b=ada_b, norm_g=norm_g,
             lru_w_x=lru_w_x, lru_b_x=lru_b_x, lru_w_y=lru_w_y, lru_b_y=lru_b_y,
             lru_conv_w=lru_conv_w, lru_conv_b=lru_conv_b, lru_w_a=lru_w_a, lru_b_a=lru_b_a,
             lru_w_i=lru_w_i, lru_b_i=lru_b_i, lru_lambda=lru_lambda, lru_w_o=lru_w_o, lru_b_o=lru_b_o,
             ssd_w_in=ssd_w_in, ssd_conv_w=ssd_conv_w, ssd_conv_b=ssd_conv_b, ssd_dt_bias=ssd_dt_bias,
             ssd_a_log=ssd_a_log, ssd_d=ssd_d, ssd_norm_g=ssd_norm_g, ssd_w_out=ssd_w_out,
             fox_w_qkv=fox_w_qkv, fox_w_f=fox_w_f, fox_b_f=fox_b_f, fox_w_o=fox_w_o,
             ffn_w_gate=ffn_w_gate, ffn_w_up=ffn_w_up, ffn_conv_w=ffn_conv_w, ffn_conv_b=ffn_conv_b,
             ffn_w_down=ffn_w_down)
    y_prompt, p = _run_trunk(x_prompt, c_prompt, None, W)
    prev = dict(l0_conv=state_l0_conv, l0_h=state_l0_h, l1_conv=state_l1_conv, l1_ssm=state_l1_ssm,
                l2_k=cache_l2_k, l2_v=cache_l2_v, l2_logf=cache_l2_logf,
                l3_conv=state_l3_conv, l3_h=state_l3_h, ffn_conv=state_ffn_conv)
    y_sample, s = _run_trunk(x_sample, c_sample, prev, W)
    return (y_prompt, y_sample,
            p['l0_conv'], p['l0_h'], p['l1_conv'], p['l1_ssm'], p['l2_k'], p['l2_v'], p['l2_logf'],
            p['l3_conv'], p['l3_h'], p['ffn_conv'],
            s['l0_conv'], s['l0_h'], s['l1_conv'], s['l1_ssm'], s['l2_k'], s['l2_v'], s['l2_logf'],
            s['l3_conv'], s['l3_h'], s['ffn_conv'])
```

```python
import functools

import jax
import jax.numpy as jnp
from jax import lax
from jax.experimental import pallas as pl
from jax.experimental.pallas import tpu as pltpu

F32 = jnp.float32
BF16 = jnp.bfloat16

EPS = 1e-6
LRU_C = 8.0
LRU_BLOCKS = 8
LRU_CONV = 4
SSD_HEAD_DIM = 64
SSD_GROUPS = 4
SSD_STATE = 128
SSD_CONV = 4
SSD_CHUNK = 64
FOX_HEADS = 16
FOX_HEAD_DIM = 64
FFN_CONV = 3

LANES = 128
SUBLANES = 8
VMEM_LIMIT = 56 * 1024 * 1024


def _const_spec(shape):
    nd = len(shape)
    return pl.BlockSpec(shape, lambda *_: (0,) * nd, pipeline_mode=pl.Buffered(1))


def _params(sem):
    return pltpu.CompilerParams(dimension_semantics=sem, vmem_limit_bytes=VMEM_LIMIT)


def _rms(x, g):
    return x * lax.rsqrt(jnp.mean(x * x, axis=-1, keepdims=True) + EPS) * g


def _dot(a, b):
    return jnp.dot(a, b, preferred_element_type=F32)


def _dot_nt(a, b):
    return lax.dot_general(a, b, (((1,), (1,)), ((), ())), preferred_element_type=F32)


def _dot_tn(a, b):
    return lax.dot_general(a, b, (((0,), (0,)), ((), ())), preferred_element_type=F32)


def _split3(x):
    hi = x.astype(BF16)
    r1 = x - hi.astype(F32)
    mid = r1.astype(BF16)
    lo = (r1 - mid.astype(F32)).astype(BF16)
    return hi, mid, lo


def _dot_exact_lhs(m, x):
    hi, mid, lo = _split3(x)
    return _dot(m, hi) + (_dot(m, mid) + _dot(m, lo))


def _expm1_nonpos(x):
    u = jnp.exp(x)
    small = (u - 1.0) * x / jnp.log(u)
    return jnp.where(x < -1.0, u - 1.0, jnp.where(u == 1.0, x, small))


def _tril(n):
    r = lax.broadcasted_iota(jnp.int32, (n, n), 0)
    c = lax.broadcasted_iota(jnp.int32, (n, n), 1)
    return r >= c


def _ada_kernel(c_ref, w_ref, b_ref, o_ref):
    c = c_ref[...]
    s = (c * jax.nn.sigmoid(c)).astype(BF16)
    o_ref[...] = _dot(s, w_ref[...].astype(BF16)) + b_ref[...]


def _ada_call(c_all, ada_w, ada_b):
    depth, d, n = ada_w.shape
    bn = c_all.shape[0]
    tn = 1536 if n % 1536 == 0 else n
    return pl.pallas_call(
        _ada_kernel,
        grid=(depth, n // tn),
        in_specs=[
            pl.BlockSpec((bn, d), lambda l, j: (0, 0)),
            pl.BlockSpec((None, d, tn), lambda l, j: (l, 0, j)),
            pl.BlockSpec((None, 1, tn), lambda l, j: (l, 0, j)),
        ],
        out_specs=pl.BlockSpec((None, bn, tn), lambda l, j: (l, 0, j)),
        out_shape=jax.ShapeDtypeStruct((depth, bn, n), F32),
        compiler_params=_params(("arbitrary", "arbitrary")),
        name="ada",
    )(c_all, ada_w, ada_b.reshape(depth, 1, n))


def _conv_from_buf(buf, cur, w_ref, b_ref, width, tt):
    y = b_ref[...] + w_ref[width - 1:width, :] * cur
    for j in range(1, width):
        y = y + w_ref[width - 1 - j:width - j, :] * buf[:, 8 - j:8 - j + tt, :]
    return y


def _ffn_kernel(x_ref, mod_ref, ng_ref, st_ref, wg_ref, wu_ref, cw_ref, cb_ref, wd_ref,
                o_ref, st_out_ref, buf, *, sb, tt):
    t = pl.program_id(1)
    d = x_ref.shape[-1]
    f = wg_ref.shape[-1]
    w = FFN_CONV

    @pl.when(t == 0)
    def _():
        buf[:, 0:8, :] = jnp.zeros((sb, 8, f), F32)
        buf[:, 8 - (w - 1):8, :] = st_ref[...]

    x = x_ref[...]
    m = mod_ref[...]
    h = _rms(x, ng_ref[2:3, :]) * (1.0 + m[:, 4:5, :]) + m[:, 3:4, :]
    hb = h.reshape(sb * tt, d).astype(BF16)
    gp = _dot(hb, wg_ref[...]).reshape(sb, tt, f)
    up = _dot(hb, wu_ref[...])
    buf[:, 8:8 + tt, :] = gp
    g = _conv_from_buf(buf, gp, cw_ref, cb_ref, w, tt)
    st_out_ref[...] = buf[:, 8 + tt - (w - 1):8 + tt, :]
    buf[:, 0:8, :] = buf[:, tt:tt + 8, :]
    act = (jax.nn.gelu(g).reshape(sb * tt, f) * up).astype(BF16)
    y = _dot(act, wd_ref[...]).reshape(sb, tt, d)
    o_ref[...] = x + m[:, 5:6, :] * _rms(y, ng_ref[3:4, :])


def _ffn_call(x, mods, ng, st, wg, wu, cw, cb, wd, sb, tt):
    bn, T, d = x.shape
    f = wg.shape[-1]
    kern = functools.partial(_ffn_kernel, sb=sb, tt=tt)
    return pl.pallas_call(
        kern,
        grid=(bn // sb, T // tt),
        in_specs=[
            pl.BlockSpec((sb, tt, d), lambda b, t: (b, t, 0)),
            pl.BlockSpec((sb, 6, d), lambda b, t: (b, 0, 0)),
            _const_spec((4, d)),
            pl.BlockSpec((sb, FFN_CONV - 1, f), lambda b, t: (b, 0, 0)),
            _const_spec((d, f)),
            _const_spec((d, f)),
            _const_spec((FFN_CONV, f)),
            _const_spec((1, f)),
            _const_spec((f, d)),
        ],
        out_specs=[
            pl.BlockSpec((sb, tt, d), lambda b, t: (b, t, 0)),
            pl.BlockSpec((sb, FFN_CONV - 1, f), lambda b, t: (b, 0, 0)),
        ],
        out_shape=[
            jax.ShapeDtypeStruct((bn, T, d), F32),
            jax.ShapeDtypeStruct((bn, FFN_CONV - 1, f), F32),
        ],
        scratch_shapes=[pltpu.VMEM((sb, tt + 8, f), F32)],
        compiler_params=_params(("arbitrary", "arbitrary")),
        name="ffn",
    )(x, mods, ng, st, wg, wu, cw, cb.reshape(1, f), wd)


def _lru_kernel(x_ref, mod_ref, ng_ref, cst_ref, hst_ref, wx_ref, bx_ref, wy_ref, by_ref,
                cw_ref, cb_ref, wai_ref, ba_ref, bi_ref, lam_ref, wo_ref, bo_ref,
                o_ref, cst_out_ref, hst_out_ref, buf, a_s, b_s, h_s, hcar, *, sb, tt):
    t = pl.program_id(1)
    d = x_ref.shape[-1]
    dr = wx_ref.shape[-1]
    bw = dr // LRU_BLOCKS
    w = LRU_CONV
    rows = sb * tt

    @pl.when(t == 0)
    def _():
        buf[:, 0:8, :] = jnp.zeros((sb, 8, dr), F32)
        buf[:, 8 - (w - 1):8, :] = cst_ref[...]
        hcar[...] = jnp.broadcast_to(hst_ref[...], (sb, 8, dr))

    x = x_ref[...]
    m = mod_ref[...]
    h = _rms(x, ng_ref[0:1, :]) * (1.0 + m[:, 1:2, :]) + m[:, 0:1, :]
    hb = h.reshape(rows, d).astype(BF16)
    xx = (_dot(hb, wx_ref[...]) + bx_ref[...]).reshape(sb, tt, dr)
    buf[:, 8:8 + tt, :] = xx
    u = _conv_from_buf(buf, xx, cw_ref, cb_ref, w, tt).reshape(rows, dr)
    cst_out_ref[...] = buf[:, 8 + tt - (w - 1):8 + tt, :]
    buf[:, 0:8, :] = buf[:, tt:tt + 8, :]

    ub = u.astype(BF16)
    ra, ia = [], []
    for n in range(LRU_BLOCKS):
        z = _dot(ub[:, n * bw:(n + 1) * bw], wai_ref[n])
        ra.append(z[:, :bw])
        ia.append(z[:, bw:])
    r = jax.nn.sigmoid(jnp.concatenate(ra, axis=1) + ba_ref[...])
    i = jax.nn.sigmoid(jnp.concatenate(ia, axis=1) + bi_ref[...])
    log_a = (-LRU_C) * r * jax.nn.softplus(-lam_ref[...])
    a_s[...] = jnp.exp(log_a)
    b_s[...] = jnp.sqrt(-_expm1_nonpos(2.0 * log_a)) * (i * u)

    row8 = lax.broadcasted_iota(jnp.int32, (8, dr), 0)
    for s in range(sb):
        def body(gi, car, s=s):
            r0 = pl.multiple_of(s * tt + gi * 8, 8)
            A = a_s[pl.ds(r0, 8), :]
            B = b_s[pl.ds(r0, 8), :]
            for sh in (1, 2, 4):
                As = pltpu.roll(A, sh, 0)
                Bs = pltpu.roll(B, sh, 0)
                msk = row8 >= sh
                B = jnp.where(msk, A * Bs + B, B)
                A = jnp.where(msk, A * As, A)
            H = A * car + B
            h_s[pl.ds(r0, 8), :] = H
            return jnp.broadcast_to(H[7:8, :], (8, dr))

        hcar[s] = lax.fori_loop(0, tt // 8, body, hcar[s])
    hst_out_ref[...] = hcar[:, 7:8, :]

    gate = jax.nn.gelu(_dot(hb, wy_ref[...]) + by_ref[...])
    yb = (h_s[...] * gate).astype(BF16)
    y = (_dot(yb, wo_ref[...]) + bo_ref[...]).reshape(sb, tt, d)
    o_ref[...] = x + m[:, 2:3, :] * _rms(y, ng_ref[1:2, :])


def _lru_call(x, mods, ng, cst, hst, p, sb, tt):
    bn, T, d = x.shape
    dr = p["wx"].shape[-1]
    bw = dr // LRU_BLOCKS
    kern = functools.partial(_lru_kernel, sb=sb, tt=tt)
    row = lambda v: v.reshape(1, -1)
    out, cst_new, hst_new = pl.pallas_call(
        kern,
        grid=(bn // sb, T // tt),
        in_specs=[
            pl.BlockSpec((sb, tt, d), lambda b, t: (b, t, 0)),
            pl.BlockSpec((sb, 6, d), lambda b, t: (b, 0, 0)),
            _const_spec((4, d)),
            pl.BlockSpec((sb, LRU_CONV - 1, dr), lambda b, t: (b, 0, 0)),
            pl.BlockSpec((sb, 1, dr), lambda b, t: (b, 0, 0)),
            _const_spec((d, dr)), _const_spec((1, dr)),
            _const_spec((d, dr)), _const_spec((1, dr)),
            _const_spec((LRU_CONV, dr)), _const_spec((1, dr)),
            _const_spec((LRU_BLOCKS, bw, 2 * bw)), _const_spec((1, dr)), _const_spec((1, dr)),
            _const_spec((1, dr)),
            _const_spec((dr, d)), _const_spec((1, d)),
        ],
        out_specs=[
            pl.BlockSpec((sb, tt, d), lambda b, t: (b, t, 0)),
            pl.BlockSpec((sb, LRU_CONV - 1, dr), lambda b, t: (b, 0, 0)),
            pl.BlockSpec((sb, 1, dr), lambda b, t: (b, 0, 0)),
        ],
        out_shape=[
            jax.ShapeDtypeStruct((bn, T, d), F32),
            jax.ShapeDtypeStruct((bn, LRU_CONV - 1, dr), F32),
            jax.ShapeDtypeStruct((bn, 1, dr), F32),
        ],
        scratch_shapes=[
            pltpu.VMEM((sb, tt + 8, dr), F32),
            pltpu.VMEM((sb * tt, dr), F32),
            pltpu.VMEM((sb * tt, dr), F32),
            pltpu.VMEM((sb * tt, dr), F32),
            pltpu.VMEM((sb, 8, dr), F32),
        ],
        compiler_params=_params(("arbitrary", "arbitrary")),
        name="lru",
    )(x, mods, ng, cst, hst.reshape(bn, 1, dr), p["wx"], row(p["bx"]), p["wy"], row(p["by"]),
      p["cw"], row(p["cb"]), p["wai"], row(p["ba"]), row(p["bi"]), row(p["lam"]),
      p["wo"], row(p["bo"]))
    return out, cst_new, hst_new.reshape(bn, dr)


def _ssd_kernel(x_ref, mod_ref, ng_ref, cst_ref, sst_ref, wz_ref, wxbc_ref, wdt_ref, cw_ref, cb_ref,
                dtb_ref, alog_ref, dsk_ref, sng_ref, wout_ref,
                o_ref, cst_out_ref, S, buf, xbc_s, dt_s, y_s, *, sb, tt):
    t = pl.program_id(1)
    d = x_ref.shape[-1]
    di = wz_ref.shape[-1]
    cdim = wxbc_ref.shape[-1]
    w = SSD_CONV
    L = SSD_CHUNK
    P = SSD_HEAD_DIM
    N = SSD_STATE
    G = SSD_GROUPS
    hpg = di // P // G
    rows = sb * tt

    @pl.when(t == 0)
    def _():
        buf[:, 0:8, :] = jnp.zeros((sb, 8, cdim), F32)
        buf[:, 8 - (w - 1):8, :] = cst_ref[...]
        S[...] = sst_ref[...]

    x = x_ref[...]
    m = mod_ref[...]
    h = _rms(x, ng_ref[0:1, :]) * (1.0 + m[:, 1:2, :]) + m[:, 0:1, :]
    hb = h.reshape(rows, d).astype(BF16)
    pre = _dot(hb, wxbc_ref[...]).reshape(sb, tt, cdim)
    buf[:, 8:8 + tt, :] = pre
    cv = _conv_from_buf(buf, pre, cw_ref, cb_ref, w, tt).reshape(rows, cdim)
    cst_out_ref[...] = buf[:, 8 + tt - (w - 1):8 + tt, :]
    buf[:, 0:8, :] = buf[:, tt:tt + 8, :]
    xbc_s[...] = cv * jax.nn.sigmoid(cv)
    dt_s[...] = jax.nn.softplus(_dot(hb, wdt_ref[...]) + dtb_ref[...])

    a_neg = -jnp.exp(alog_ref[...])
    tril = _tril(L)
    tril_b = jnp.where(tril, 1.0, 0.0).astype(BF16)

    nck = tt // L

    def chunk(idx, carry):
        s = idx // nck
        r0 = pl.multiple_of(idx * L, L)
        dt = dt_s[pl.ds(r0, L), :]
        cum = _dot_exact_lhs(tril_b, dt * a_neg)
        cum_t = cum.T
        dt_t = dt.T
        clast = cum[L - 1:L, :]
        wend = jnp.exp(clast - cum) * dt
        ecum = jnp.exp(cum)
        eclast = jnp.exp(clast)
        for g in range(G):
            Bm = xbc_s[pl.ds(r0, L), di + g * N:di + (g + 1) * N]
            Cm = xbc_s[pl.ds(r0, L), di + G * N + g * N:di + G * N + (g + 1) * N]
            cb = _dot_nt(Cm.astype(BF16), Bm.astype(BF16))
            for e in range(hpg):
                hh = g * hpg + e
                xs = xbc_s[pl.ds(r0, L), hh * P:(hh + 1) * P].astype(BF16)
                seg = cum[:, hh:hh + 1] - cum_t[hh:hh + 1, :]
                dec = jnp.exp(jnp.where(tril, seg, -jnp.inf))
                mm = (cb * dec * dt_t[hh:hh + 1, :]).astype(BF16)
                yd = _dot(mm, xs)
                Sh = S[s, hh * P:(hh + 1) * P, :]
                cs = (Cm * ecum[:, hh:hh + 1]).astype(BF16)
                yo = _dot_nt(cs, Sh.astype(BF16))
                bwt = (Bm * wend[:, hh:hh + 1]).astype(BF16)
                st = _dot_tn(xs, bwt)
                S[s, hh * P:(hh + 1) * P, :] = Sh * eclast[:, hh:hh + 1] + st
                y_s[pl.ds(r0, L), hh * P:(hh + 1) * P] = yd + yo
        return carry

    lax.fori_loop(0, sb * nck, chunk, 0)

    z = _dot(hb, wz_ref[...])
    y = y_s[...] + dsk_ref[...] * xbc_s[:, 0:di]
    y = _rms(y * (z * jax.nn.sigmoid(z)), sng_ref[...])
    yo = _dot(y.astype(BF16), wout_ref[...]).reshape(sb, tt, d)
    o_ref[...] = x + m[:, 2:3, :] * _rms(yo, ng_ref[1:2, :])


def _ssd_call(x, mods, ng, cst, sst, p, sb, tt):
    bn, T, d = x.shape
    di = p["wz"].shape[-1]
    cdim = p["wxbc"].shape[-1]
    nh = di // SSD_HEAD_DIM
    kern = functools.partial(_ssd_kernel, sb=sb, tt=tt)
    out, cst_new, sst_new = pl.pallas_call(
        kern,
        grid=(bn // sb, T // tt),
        in_specs=[
            pl.BlockSpec((sb, tt, d), lambda b, t: (b, t, 0)),
            pl.BlockSpec((sb, 6, d), lambda b, t: (b, 0, 0)),
            _const_spec((4, d)),
            pl.BlockSpec((sb, SSD_CONV - 1, cdim), lambda b, t: (b, 0, 0)),
            pl.BlockSpec((sb, di, SSD_STATE), lambda b, t: (b, 0, 0)),
            _const_spec((d, di)), _const_spec((d, cdim)), _const_spec((d, LANES)),
            _const_spec((SSD_CONV, cdim)), _const_spec((1, cdim)),
            _const_spec((1, LANES)), _const_spec((1, LANES)),
            _const_spec((1, di)), _const_spec((1, di)),
            _const_spec((di, d)),
        ],
        out_specs=[
            pl.BlockSpec((sb, tt, d), lambda b, t: (b, t, 0)),
            pl.BlockSpec((sb, SSD_CONV - 1, cdim), lambda b, t: (b, 0, 0)),
            pl.BlockSpec((sb, di, SSD_STATE), lambda b, t: (b, 0, 0)),
        ],
        out_shape=[
            jax.ShapeDtypeStruct((bn, T, d), F32),
            jax.ShapeDtypeStruct((bn, SSD_CONV - 1, cdim), F32),
            jax.ShapeDtypeStruct((bn, di, SSD_STATE), F32),
        ],
        scratch_shapes=[
            pltpu.VMEM((sb, tt + 8, cdim), F32),
            pltpu.VMEM((sb * tt, cdim), F32),
            pltpu.VMEM((sb * tt, LANES), F32),
            pltpu.VMEM((sb * tt, di), F32),
        ],
        compiler_params=_params(("arbitrary", "arbitrary")),
        name="ssd",
    )(x, mods, ng, cst, sst.reshape(bn, di, SSD_STATE), p["wz"], p["wxbc"], p["wdt"], p["cw"], p["cb"],
      p["dtb"], p["alog"], p["dsk"], p["sng"], p["wout"])
    return out, cst_new, sst_new.reshape(bn, nh, SSD_HEAD_DIM, SSD_STATE)


def _fox_proj_kernel(*refs, sb, tt, past):
    if past:
        (x_ref, mod_ref, ng_ref, lfp_ref, wqkv_ref, wf_ref, bf_ref,
         q_ref, k_ref, v_ref, kb_ref, vb_ref, lf_ref, fq_ref, fp_ref, fcar) = refs
    else:
        (x_ref, mod_ref, ng_ref, wqkv_ref, wf_ref, bf_ref,
         q_ref, k_ref, v_ref, kb_ref, vb_ref, lf_ref, fq_ref, fcar) = refs
    t = pl.program_id(1)
    d = x_ref.shape[-1]
    nh = lf_ref.shape[-1]
    rows = sb * tt
    cblk = min(tt, 256)
    trt = jnp.where(_tril(cblk), 1.0, 0.0).astype(BF16)

    def cumsum_rows(get_rows, n, car):
        outs = []
        for r in range(0, n, cblk):
            c = _dot_exact_lhs(trt, get_rows(r, cblk)) + car
            outs.append(c)
            car = c[cblk - 1:cblk, :]
        return outs, car

    @pl.when(t == 0)
    def _():
        if past:
            plen = lfp_ref.shape[1]
            for s in range(sb):
                outs, car = cumsum_rows(lambda r, n, s=s: lfp_ref[s, r:r + n, :], plen, jnp.zeros((1, LANES), F32))
                for bi, c in enumerate(outs):
                    fp_ref[s, bi * cblk:(bi + 1) * cblk, :] = c[:, 0:nh]
                fcar[s] = jnp.broadcast_to(car, (8, LANES))
        else:
            fcar[...] = jnp.zeros((sb, 8, LANES), F32)

    x = x_ref[...]
    m = mod_ref[...]
    h = _rms(x, ng_ref[0:1, :]) * (1.0 + m[:, 1:2, :]) + m[:, 0:1, :]
    hb = h.reshape(rows, d).astype(BF16)
    qkv = _dot(hb, wqkv_ref[...])
    q = qkv[:, 0:d] * (FOX_HEAD_DIM ** -0.5)
    k = qkv[:, d:2 * d]
    v = qkv[:, 2 * d:3 * d]
    q_ref[...] = q.astype(BF16).reshape(sb, tt, d)
    k_ref[...] = k.reshape(sb, tt, d)
    v_ref[...] = v.reshape(sb, tt, d)
    kb_ref[...] = k.astype(BF16).reshape(sb, tt, d)
    vb_ref[...] = v.astype(BF16).reshape(sb, tt, d)
    lf = jax.nn.log_sigmoid(_dot(hb, wf_ref[...]) + bf_ref[...])
    lf_ref[...] = lf[:, 0:nh].reshape(sb, tt, nh)
    for s in range(sb):
        outs, car = cumsum_rows(lambda r, n, s=s: lf[s * tt + r:s * tt + r + n, :], tt, fcar[s, 0:1, :])
        for bi, c in enumerate(outs):
            fq_ref[s, bi * cblk:(bi + 1) * cblk, :] = c[:, 0:nh]
        fcar[s] = jnp.broadcast_to(car, (8, LANES))


def _fox_proj_call(x, mods, ng, lf_past, wqkv, wf, bf, sb, tt):
    bn, T, d = x.shape
    nh = FOX_HEADS
    past = lf_past is not None
    kern = functools.partial(_fox_proj_kernel, sb=sb, tt=tt, past=past)
    in_specs = [
        pl.BlockSpec((sb, tt, d), lambda b, t: (b, t, 0)),
        pl.BlockSpec((sb, 6, d), lambda b, t: (b, 0, 0)),
        _const_spec((4, d)),
    ]
    args = [x, mods, ng]
    if past:
        plen = lf_past.shape[1]
        in_specs.append(pl.BlockSpec((sb, plen, LANES), lambda b, t: (b, 0, 0)))
        args.append(jnp.pad(lf_past, ((0, 0), (0, 0), (0, LANES - nh))))
    in_specs += [_const_spec((d, 3 * d)), _const_spec((d, LANES)), _const_spec((1, LANES))]
    args += [wqkv, wf, jnp.pad(bf.reshape(1, nh), ((0, 0), (0, LANES - nh)))]
    tile = pl.BlockSpec((sb, tt, d), lambda b, t: (b, t, 0))
    small = pl.BlockSpec((sb, tt, nh), lambda b, t: (b, t, 0))
    out_specs = [tile, tile, tile, tile, tile, small, small]
    out_shape = [
        jax.ShapeDtypeStruct((bn, T, d), BF16),
        jax.ShapeDtypeStruct((bn, T, d), F32),
        jax.ShapeDtypeStruct((bn, T, d), F32),
        jax.ShapeDtypeStruct((bn, T, d), BF16),
        jax.ShapeDtypeStruct((bn, T, d), BF16),
        jax.ShapeDtypeStruct((bn, T, nh), F32),
        jax.ShapeDtypeStruct((bn, T, nh), F32),
    ]
    if past:
        out_specs.append(pl.BlockSpec((sb, plen, nh), lambda b, t: (b, 0, 0)))
        out_shape.append(jax.ShapeDtypeStruct((bn, plen, nh), F32))
    return pl.pallas_call(
        kern,
        grid=(bn // sb, T // tt),
        in_specs=in_specs,
        out_specs=out_specs,
        out_shape=out_shape,
        scratch_shapes=[pltpu.VMEM((sb, 8, LANES), F32)],
        compiler_params=_params(("arbitrary", "arbitrary")),
        name="fox_proj",
    )(*args)


def _fox_attn_kernel(x_ref, mod_ref, ng_ref, q_ref, k_ref, v_ref, fq_ref, fkt_ref, wo_ref,
                     o_ref, acc, m_s, l_s, *, tq, tk, q_off, k_pad):
    qi = pl.program_id(1)
    ki = pl.program_id(2)
    nk = pl.num_programs(2)
    d = x_ref.shape[-1]
    npair = d // LANES
    hd = FOX_HEAD_DIM
    q_lo = q_off + qi * tq
    k_lo = ki * tk

    @pl.when(ki == 0)
    def _():
        acc[...] = jnp.zeros(acc.shape, F32)
        m_s[...] = jnp.full(m_s.shape, -jnp.inf, F32)
        l_s[...] = jnp.zeros(l_s.shape, F32)

    lane = lax.broadcasted_iota(jnp.int32, (tq, LANES), 1)
    low = lane < hd

    def step(masked):
        if masked:
            qpos = q_lo + lax.broadcasted_iota(jnp.int32, (tq, tk), 0)
            kpos = k_lo + lax.broadcasted_iota(jnp.int32, (tq, tk), 1)
            ok = (kpos <= qpos) & (kpos >= k_pad)
        for j in range(npair):
            q2 = q_ref[:, j * LANES:(j + 1) * LANES]
            k2 = k_ref[:, j * LANES:(j + 1) * LANES]
            v2 = v_ref[:, j * LANES:(j + 1) * LANES]
            outs, alphas = [], []
            for half in range(2):
                hh = 2 * j + half
                qh = jnp.where(low if half == 0 else ~low, q2, jnp.zeros_like(q2))
                fq = fq_ref[:, hh:hh + 1]
                tv = _dot_nt(qh, k2) - fkt_ref[hh:hh + 1, :]
                if masked:
                    tv = jnp.where(ok, tv, -jnp.inf)
                m_old = m_s[hh]
                m_new = jnp.maximum(m_old, jnp.max(tv, axis=-1, keepdims=True) + fq)
                m_safe = jnp.where(m_new == -jnp.inf, 0.0, m_new)
                p = jnp.exp(tv - (m_safe - fq))
                alpha = jnp.exp(m_old - m_safe)
                l_s[hh] = alpha * l_s[hh] + jnp.sum(p, axis=-1, keepdims=True)
                m_s[hh] = m_new
                outs.append(_dot(p.astype(BF16), v2))
                alphas.append(alpha)
            a2 = jnp.where(low, alphas[0], alphas[1])
            acc[:, j * LANES:(j + 1) * LANES] = (a2 * acc[:, j * LANES:(j + 1) * LANES]
                                                 + jnp.where(low, outs[0], outs[1]))

    needed = k_lo <= q_lo + tq - 1
    need_mask = (k_lo + tk - 1 > q_lo) | (k_lo < k_pad)

    @pl.when(needed & need_mask)
    def _():
        step(True)

    @pl.when(needed & jnp.logical_not(need_mask))
    def _():
        step(False)

    @pl.when(ki == nk - 1)
    def _():
        cols = []
        for j in range(npair):
            inv = jnp.where(low, 1.0 / l_s[2 * j], 1.0 / l_s[2 * j + 1])
            cols.append((acc[:, j * LANES:(j + 1) * LANES] * inv).astype(BF16))
        o = jnp.concatenate(cols, axis=1)
        y = _dot(o, wo_ref[...])
        m = mod_ref[...]
        o_ref[...] = x_ref[...] + m[2:3, :] * _rms(y, ng_ref[1:2, :])


def _fox_attn_call(x, mods, ng, q, kb, vb, fq, fkt, wo, tq, tk, q_off, k_pad):
    bn, T, d = x.shape
    tkeys = kb.shape[1]
    nh = FOX_HEADS
    kern = functools.partial(_fox_attn_kernel, tq=tq, tk=tk, q_off=q_off, k_pad=k_pad)

    def kmap(b, i, j):
        last = (q_off + i * tq + tq - 1) // tk
        return (b, jnp.minimum(j, last), 0)

    def fkmap(b, i, j):
        last = (q_off + i * tq + tq - 1) // tk
        return (b, 0, jnp.minimum(j, last))

    return pl.pallas_call(
        kern,
        grid=(bn, T // tq, tkeys // tk),
        in_specs=[
            pl.BlockSpec((None, tq, d), lambda b, i, j: (b, i, 0)),
            pl.BlockSpec((None, 6, d), lambda b, i, j: (b, 0, 0)),
            _const_spec((4, d)),
            pl.BlockSpec((None, tq, d), lambda b, i, j: (b, i, 0)),
            pl.BlockSpec((None, tk, d), kmap),
            pl.BlockSpec((None, tk, d), kmap),
            pl.BlockSpec((None, tq, nh), lambda b, i, j: (b, i, 0)),
            pl.BlockSpec((None, nh, tk), fkmap),
            _const_spec((d, d)),
        ],
        out_specs=pl.BlockSpec((None, tq, d), lambda b, i, j: (b, i, 0)),
        out_shape=jax.ShapeDtypeStruct((bn, T, d), F32),
        scratch_shapes=[
            pltpu.VMEM((tq, d), F32),
            pltpu.VMEM((nh, tq, 1), F32),
            pltpu.VMEM((nh, tq, 1), F32),
        ],
        compiler_params=_params(("arbitrary", "arbitrary", "arbitrary")),
        name="fox_attn",
    )(x, mods, ng, q, kb, vb, fq, fkt, wo)


def _prep_weights(W):
    bf = lambda a: a.astype(BF16)
    di = W["ssd_w_out"].shape[0]
    nh = di // SSD_HEAD_DIM
    cdim = W["ssd_conv_w"].shape[-1]
    d = W["ssd_w_in"].shape[0]
    lru = []
    for j in range(W["lru_w_x"].shape[0]):
        lru.append(dict(
            wx=bf(W["lru_w_x"][j]), bx=W["lru_b_x"][j], wy=bf(W["lru_w_y"][j]), by=W["lru_b_y"][j],
            cw=W["lru_conv_w"][j], cb=W["lru_conv_b"][j],
            wai=bf(jnp.concatenate([W["lru_w_a"][j], W["lru_w_i"][j]], axis=-1)),
            ba=W["lru_b_a"][j], bi=W["lru_b_i"][j], lam=W["lru_lambda"][j],
            wo=bf(W["lru_w_o"][j]), bo=W["lru_b_o"][j]))
    w_in = W["ssd_w_in"]
    pad_l = lambda v: jnp.pad(v.reshape(1, -1), ((0, 0), (0, LANES - v.shape[-1])))
    ssd = dict(
        wz=bf(w_in[:, :di]), wxbc=bf(w_in[:, di:di + cdim]),
        wdt=bf(jnp.pad(w_in[:, di + cdim:], ((0, 0), (0, LANES - nh)))),
        cw=W["ssd_conv_w"], cb=W["ssd_conv_b"].reshape(1, cdim),
        dtb=pad_l(W["ssd_dt_bias"]), alog=pad_l(W["ssd_a_log"]),
        dsk=jnp.repeat(W["ssd_d"], SSD_HEAD_DIM).reshape(1, di), sng=W["ssd_norm_g"].reshape(1, di),
        wout=bf(W["ssd_w_out"]))
    fox = dict(
        wqkv=bf(W["fox_w_qkv"]),
        wf=bf(jnp.pad(W["fox_w_f"], ((0, 0), (0, LANES - W["fox_w_f"].shape[-1])))),
        bf=W["fox_b_f"], wo=bf(W["fox_w_o"]))
    ffn = [dict(wg=bf(W["ffn_w_gate"][l]), wu=bf(W["ffn_w_up"][l]), cw=W["ffn_conv_w"][l],
                cb=W["ffn_conv_b"][l], wd=bf(W["ffn_w_down"][l])) for l in range(W["ffn_w_gate"].shape[0])]
    return dict(lru=lru, ssd=ssd, fox=fox, ffn=ffn, norm_g=W["norm_g"])


def _run_trunk(x, mods, prev, P, cfg):
    bn, T, d = x.shape
    sb, tt, tt_ffn, tq, tk = cfg["sb"], cfg["tt"], cfg["tt_ffn"], cfg["tq"], cfg["tk"]
    depth = mods.shape[0]
    new = {}
    ffn_bufs = []
    for layer in range(depth):
        ng = P["norm_g"][layer]
        ml = mods[layer]
        kind = layer % 3
        tag = "l%d" % layer
        if kind == 0:
            x, cn, hn = _lru_call(x, ml, ng, prev[tag + "_conv"], prev[tag + "_h"], P["lru"][layer // 3], sb, tt)
            new[tag + "_conv"], new[tag + "_h"] = cn, hn
        elif kind == 1:
            x, cn, sn = _ssd_call(x, ml, ng, prev[tag + "_conv"], prev[tag + "_ssm"], P["ssd"], sb, tt)
            new[tag + "_conv"], new[tag + "_ssm"] = cn, sn
        else:
            fx = P["fox"]
            kp, vp, lp = prev.get(tag + "_k"), prev.get(tag + "_v"), prev.get(tag + "_logf")
            res = _fox_proj_call(x, ml, ng, lp, fx["wqkv"], fx["wf"], fx["bf"], sb, tt)
            q, k, v, kb, vb, lf, fq = res[:7]
            if kp is None:
                fk, q_off, k_pad = fq, 0, 0
            else:
                plen = kp.shape[1]
                tk = -(-(plen + T) // LANES) * LANES
                k_pad = tk - (plen + T)
                zk = jnp.zeros((bn, k_pad, d), BF16)
                kb = jnp.concatenate([zk, kp.reshape(bn, plen, d).astype(BF16), kb], axis=1)
                vb = jnp.concatenate([zk, vp.reshape(bn, plen, d).astype(BF16), vb], axis=1)
                fk = jnp.concatenate([jnp.zeros((bn, k_pad, FOX_HEADS), F32), res[7], fq], axis=1)
                q_off = k_pad + plen
            fkt = jnp.swapaxes(fk, 1, 2)
            x = _fox_attn_call(x, ml, ng, q, kb, vb, fq, fkt, fx["wo"], tq, tk, q_off, k_pad)
            new[tag + "_k"] = k.reshape(bn, T, FOX_HEADS, FOX_HEAD_DIM)
            new[tag + "_v"] = v.reshape(bn, T, FOX_HEADS, FOX_HEAD_DIM)
            new[tag + "_logf"] = lf
        f = P["ffn"][layer]
        x, buf = _ffn_call(x, ml, ng, prev["ffn_conv"][layer], f["wg"], f["wu"], f["cw"], f["cb"], f["wd"],
                           sb, tt_ffn)
        ffn_bufs.append(buf)
    new["ffn_conv"] = jnp.stack(ffn_bufs)
    return x, new


def _fresh_state(bn, W):
    dr = W["lru_w_x"].shape[-1]
    cdim = W["ssd_conv_w"].shape[-1]
    di = W["ssd_w_out"].shape[0]
    f = W["ffn_w_gate"].shape[-1]
    depth = W["ffn_w_gate"].shape[0]
    st = dict(ffn_conv=jnp.zeros((depth, bn, FFN_CONV - 1, f), F32))
    for layer in range(depth):
        tag = "l%d" % layer
        if layer % 3 == 0:
            st[tag + "_conv"] = jnp.zeros((bn, LRU_CONV - 1, dr), F32)
            st[tag + "_h"] = jnp.zeros((bn, dr), F32)
        elif layer % 3 == 1:
            st[tag + "_conv"] = jnp.zeros((bn, SSD_CONV - 1, cdim), F32)
            st[tag + "_ssm"] = jnp.zeros((bn, di // SSD_HEAD_DIM, SSD_HEAD_DIM, SSD_STATE), F32)
    return st


def _tile_cfg(bn, T):
    if T >= 256:
        return dict(sb=1, tt=256, tt_ffn=256, tq=512 if T % 512 == 0 else 256, tk=512 if T % 512 == 0 else 256)
    sb = 4 if bn % 4 == 0 else 1
    return dict(sb=sb, tt=T, tt_ffn=T, tq=T, tk=128)


def kernel(x_prompt, x_sample, c_prompt, c_sample, state_l0_conv, state_l0_h, state_l1_conv, state_l1_ssm, cache_l2_k, cache_l2_v, cache_l2_logf, state_l3_conv, state_l3_h, state_ffn_conv, ada_w, ada_b, norm_g, lru_w_x, lru_b_x, lru_w_y, lru_b_y, lru_conv_w, lru_conv_b, lru_w_a, lru_b_a, lru_w_i, lru_b_i, lru_lambda, lru_w_o, lru_b_o, ssd_w_in, ssd_conv_w, ssd_conv_b, ssd_dt_bias, ssd_a_log, ssd_d, ssd_norm_g, ssd_w_out, fox_w_qkv, fox_w_f, fox_b_f, fox_w_o, ffn_w_gate, ffn_w_up, ffn_conv_w, ffn_conv_b, ffn_w_down):
    W = dict(ada_w=ada_w, ada_b=ada_b, norm_g=norm_g,
             lru_w_x=lru_w_x, lru_b_x=lru_b_x, lru_w_y=lru_w_y, lru_b_y=lru_b_y,
             lru_conv_w=lru_conv_w, lru_conv_b=lru_conv_b, lru_w_a=lru_w_a, lru_b_a=lru_b_a,
             lru_w_i=lru_w_i, lru_b_i=lru_b_i, lru_lambda=lru_lambda, lru_w_o=lru_w_o, lru_b_o=lru_b_o,
             ssd_w_in=ssd_w_in, ssd_conv_w=ssd_conv_w, ssd_conv_b=ssd_conv_b, ssd_dt_bias=ssd_dt_bias,
             ssd_a_log=ssd_a_log, ssd_d=ssd_d, ssd_norm_g=ssd_norm_g, ssd_w_out=ssd_w_out,
             fox_w_qkv=fox_w_qkv, fox_w_f=fox_w_f, fox_b_f=fox_b_f, fox_w_o=fox_w_o,
             ffn_w_gate=ffn_w_gate, ffn_w_up=ffn_w_up, ffn_conv_w=ffn_conv_w, ffn_conv_b=ffn_conv_b,
             ffn_w_down=ffn_w_down)
    P = _prep_weights(W)
    bp, bs = x_prompt.shape[0], x_sample.shape[0]
    d = x_prompt.shape[-1]
    depth = ada_w.shape[0]
    mods = _ada_call(jnp.concatenate([c_prompt, c_sample], axis=0), ada_w, ada_b)
    mods = mods.reshape(depth, bp + bs, 6, d)

    y_prompt, p = _run_trunk(x_prompt, mods[:, :bp], _fresh_state(bp, W), P, _tile_cfg(bp, x_prompt.shape[1]))
    prev = dict(l0_conv=state_l0_conv, l0_h=state_l0_h, l1_conv=state_l1_conv, l1_ssm=state_l1_ssm,
                l2_k=cache_l2_k, l2_v=cache_l2_v, l2_logf=cache_l2_logf,
                l3_conv=state_l3_conv, l3_h=state_l3_h, ffn_conv=state_ffn_conv)
    y_sample, s = _run_trunk(x_sample, mods[:, bp:], prev, P, _tile_cfg(bs, x_sample.shape[1]))
    return (y_prompt, y_sample,
            p['l0_conv'], p['l0_h'], p['l1_conv'], p['l1_ssm'], p['l2_k'], p['l2_v'], p['l2_logf'],
            p['l3_conv'], p['l3_h'], p['ffn_conv'],
            s['l0_conv'], s['l0_h'], s['l1_conv'], s['l1_ssm'], s['l2_k'], s['l2_v'], s['l2_logf'],
            s['l3_conv'], s['l3_h'], s['ffn_conv'])
```

```python
import functools

import jax
import jax.numpy as jnp
from jax import lax
from jax.experimental import pallas as pl
from jax.experimental.pallas import tpu as pltpu

F32 = jnp.float32
BF16 = jnp.bfloat16

EPS = 1e-6
LOG2E = 1.4426950408889634
LRU_C = 8.0
LRU_BLOCKS = 8
LRU_CONV = 4
SSD_HEAD_DIM = 64
SSD_GROUPS = 4
SSD_STATE = 128
SSD_CONV = 4
SSD_CHUNK = 64
FOX_HEADS = 16
FOX_HEAD_DIM = 64
FFN_CONV = 3

LANES = 128
SUBLANES = 8
VMEM_LIMIT = 56 * 1024 * 1024


def _const_spec(shape):
    nd = len(shape)
    return pl.BlockSpec(shape, lambda *_: (0,) * nd, pipeline_mode=pl.Buffered(1))


def _params(sem):
    return pltpu.CompilerParams(dimension_semantics=sem, vmem_limit_bytes=VMEM_LIMIT)


def _rms(x, g):
    return x * lax.rsqrt(jnp.mean(x * x, axis=-1, keepdims=True) + EPS) * g


def _dot(a, b):
    return jnp.dot(a, b, preferred_element_type=F32)


def _dot_nt(a, b):
    return lax.dot_general(a, b, (((1,), (1,)), ((), ())), preferred_element_type=F32)


def _dot_tn(a, b):
    return lax.dot_general(a, b, (((0,), (0,)), ((), ())), preferred_element_type=F32)


def _split3(x):
    hi = x.astype(BF16)
    r1 = x - hi.astype(F32)
    mid = r1.astype(BF16)
    lo = (r1 - mid.astype(F32)).astype(BF16)
    return hi, mid, lo


def _dot_exact_lhs(m, x):
    hi, mid, lo = _split3(x)
    return _dot(m, hi) + (_dot(m, mid) + _dot(m, lo))


def _expm1_nonpos(x):
    u = jnp.exp(x)
    small = (u - 1.0) * x / jnp.log(u)
    return jnp.where(x < -1.0, u - 1.0, jnp.where(u == 1.0, x, small))


def _gelu_tanh(x):
    c0 = 0.7978845608028654
    hx = 0.5 * x
    return hx + hx * jnp.tanh(x * (c0 + (c0 * 0.044715) * (x * x)))


def _tril(n):
    r = lax.broadcasted_iota(jnp.int32, (n, n), 0)
    c = lax.broadcasted_iota(jnp.int32, (n, n), 1)
    return r >= c


def _ada_kernel(c_ref, w_ref, b_ref, o_ref):
    c = c_ref[...]
    s = (c * jax.nn.sigmoid(c)).astype(BF16)
    o_ref[...] = _dot(s, w_ref[...].astype(BF16)) + b_ref[...]


def _ada_call(c_all, ada_w, ada_b):
    depth, d, n = ada_w.shape
    bn = c_all.shape[0]
    tn = 1536 if n % 1536 == 0 else n
    return pl.pallas_call(
        _ada_kernel,
        grid=(depth, n // tn),
        in_specs=[
            pl.BlockSpec((bn, d), lambda l, j: (0, 0)),
            pl.BlockSpec((None, d, tn), lambda l, j: (l, 0, j)),
            pl.BlockSpec((None, 1, tn), lambda l, j: (l, 0, j)),
        ],
        out_specs=pl.BlockSpec((None, bn, tn), lambda l, j: (l, 0, j)),
        out_shape=jax.ShapeDtypeStruct((depth, bn, n), F32),
        compiler_params=_params(("arbitrary", "arbitrary")),
        name="ada",
    )(c_all, ada_w, ada_b.reshape(depth, 1, n))


def _conv_from_buf(buf, cur, w_ref, b_ref, width, tt):
    y = b_ref[...] + w_ref[width - 1:width, :] * cur
    for j in range(1, width):
        y = y + w_ref[width - 1 - j:width - j, :] * buf[:, 8 - j:8 - j + tt, :]
    return y


def _ffn_kernel(x_ref, mod_ref, ng_ref, st_ref, wg_ref, wu_ref, cw_ref, cb_ref, wd_ref,
                o_ref, st_out_ref, buf, *, sb, tt):
    t = pl.program_id(1)
    d = x_ref.shape[-1]
    f = wg_ref.shape[-1]
    w = FFN_CONV

    @pl.when(t == 0)
    def _():
        buf[:, 0:8, :] = jnp.zeros((sb, 8, f), F32)
        buf[:, 8 - (w - 1):8, :] = st_ref[...]

    x = x_ref[...]
    m = mod_ref[...]
    h = _rms(x, ng_ref[2:3, :]) * (1.0 + m[:, 4:5, :]) + m[:, 3:4, :]
    hb = h.reshape(sb * tt, d).astype(BF16)
    gp = _dot(hb, wg_ref[...]).reshape(sb, tt, f)
    up = _dot(hb, wu_ref[...])
    buf[:, 8:8 + tt, :] = gp
    g = _conv_from_buf(buf, gp, cw_ref, cb_ref, w, tt)
    st_out_ref[...] = buf[:, 8 + tt - (w - 1):8 + tt, :]
    buf[:, 0:8, :] = buf[:, tt:tt + 8, :]
    act = (_gelu_tanh(g).reshape(sb * tt, f) * up).astype(BF16)
    y = _dot(act, wd_ref[...]).reshape(sb, tt, d)
    o_ref[...] = x + m[:, 5:6, :] * _rms(y, ng_ref[3:4, :])


def _ffn_call(x, mods, ng, st, wg, wu, cw, cb, wd, sb, tt):
    bn, T, d = x.shape
    f = wg.shape[-1]
    kern = functools.partial(_ffn_kernel, sb=sb, tt=tt)
    return pl.pallas_call(
        kern,
        grid=(bn // sb, T // tt),
        in_specs=[
            pl.BlockSpec((sb, tt, d), lambda b, t: (b, t, 0)),
            pl.BlockSpec((sb, 6, d), lambda b, t: (b, 0, 0)),
            _const_spec((4, d)),
            pl.BlockSpec((sb, FFN_CONV - 1, f), lambda b, t: (b, 0, 0)),
            _const_spec((d, f)),
            _const_spec((d, f)),
            _const_spec((FFN_CONV, f)),
            _const_spec((1, f)),
            _const_spec((f, d)),
        ],
        out_specs=[
            pl.BlockSpec((sb, tt, d), lambda b, t: (b, t, 0)),
            pl.BlockSpec((sb, FFN_CONV - 1, f), lambda b, t: (b, 0, 0)),
        ],
        out_shape=[
            jax.ShapeDtypeStruct((bn, T, d), F32),
            jax.ShapeDtypeStruct((bn, FFN_CONV - 1, f), F32),
        ],
        scratch_shapes=[pltpu.VMEM((sb, tt + 8, f), F32)],
        compiler_params=_params(("arbitrary", "arbitrary")),
        name="ffn",
    )(x, mods, ng, st, wg, wu, cw, cb.reshape(1, f), wd)


def _lru_kernel(x_ref, mod_ref, ng_ref, cst_ref, hst_ref, wx_ref, bx_ref, wy_ref, by_ref,
                cw_ref, cb_ref, wai_ref, ba_ref, bi_ref, lam_ref, wo_ref, bo_ref,
                o_ref, cst_out_ref, hst_out_ref, buf, za_s, zi_s, u_s, g_s, hcar, *, sb, tt):
    t = pl.program_id(1)
    d = x_ref.shape[-1]
    dr = wx_ref.shape[-1]
    bw = dr // LRU_BLOCKS
    w = LRU_CONV
    rows = sb * tt

    @pl.when(t == 0)
    def _():
        buf[:, 0:8, :] = jnp.zeros((sb, 8, dr), F32)
        buf[:, 8 - (w - 1):8, :] = cst_ref[...]
        hcar[...] = jnp.broadcast_to(hst_ref[...], (sb, 8, dr))

    x = x_ref[...]
    m = mod_ref[...]
    h = _rms(x, ng_ref[0:1, :]) * (1.0 + m[:, 1:2, :]) + m[:, 0:1, :]
    hb = h.reshape(rows, d).astype(BF16)
    xx = (_dot(hb, wx_ref[...]) + bx_ref[...]).reshape(sb, tt, dr)
    buf[:, 8:8 + tt, :] = xx
    u = _conv_from_buf(buf, xx, cw_ref, cb_ref, w, tt).reshape(rows, dr)
    cst_out_ref[...] = buf[:, 8 + tt - (w - 1):8 + tt, :]
    buf[:, 0:8, :] = buf[:, tt:tt + 8, :]

    u_s[...] = u
    ub = u.astype(BF16)
    for n in range(LRU_BLOCKS):
        z = _dot(ub[:, n * bw:(n + 1) * bw], wai_ref[n])
        za_s[:, n * bw:(n + 1) * bw] = z[:, :bw]
        zi_s[:, n * bw:(n + 1) * bw] = z[:, bw:]
    g_s[...] = _dot(hb, wy_ref[...]) + by_ref[...]

    lc = 256
    row8 = lax.broadcasted_iota(jnp.int32, (8, lc), 0)
    khalf = (-0.5 * LRU_C) * jax.nn.softplus(-lam_ref[...])
    for s in range(sb):
        def body(gi, car, s=s):
            r0 = pl.multiple_of(s * tt + gi * 8, 8)
            new_car = []
            for c in range(dr // lc):
                ls = slice(c * lc, (c + 1) * lc)
                kh = khalf[:, ls]
                log_a = kh + kh * jnp.tanh(za_s[pl.ds(r0, 8), ls] + ba_ref[:, ls])
                hu = 0.5 * u_s[pl.ds(r0, 8), ls]
                iu = hu + hu * jnp.tanh(zi_s[pl.ds(r0, 8), ls] + bi_ref[:, ls])
                A = jnp.exp(log_a)
                B = jnp.sqrt(jnp.tanh(-log_a) * (1.0 + A * A)) * iu
                for sh in (1, 2, 4):
                    As = pltpu.roll(A, sh, 0)
                    Bs = pltpu.roll(B, sh, 0)
                    msk = row8 >= sh
                    B = jnp.where(msk, A * Bs + B, B)
                    A = jnp.where(msk, A * As, A)
                H = A * car[c] + B
                g_s[pl.ds(r0, 8), ls] = H * _gelu_tanh(g_s[pl.ds(r0, 8), ls])
                new_car.append(jnp.broadcast_to(H[7:8, :], (8, lc)))
            return tuple(new_car)

        car0 = tuple(hcar[s, :, c * lc:(c + 1) * lc] for c in range(dr // lc))
        car = lax.fori_loop(0, tt // 8, body, car0, unroll=2)
        for c in range(dr // lc):
            hcar[s, :, c * lc:(c + 1) * lc] = car[c]
    hst_out_ref[...] = hcar[:, 7:8, :]

    y = (_dot(g_s[...].astype(BF16), wo_ref[...]) + bo_ref[...]).reshape(sb, tt, d)
    o_ref[...] = x + m[:, 2:3, :] * _rms(y, ng_ref[1:2, :])


def _lru_call(x, mods, ng, cst, hst, p, sb, tt):
    bn, T, d = x.shape
    dr = p["wx"].shape[-1]
    bw = dr // LRU_BLOCKS
    kern = functools.partial(_lru_kernel, sb=sb, tt=tt)
    row = lambda v: v.reshape(1, -1)
    out, cst_new, hst_new = pl.pallas_call(
        kern,
        grid=(bn // sb, T // tt),
        in_specs=[
            pl.BlockSpec((sb, tt, d), lambda b, t: (b, t, 0)),
            pl.BlockSpec((sb, 6, d), lambda b, t: (b, 0, 0)),
            _const_spec((4, d)),
            pl.BlockSpec((sb, LRU_CONV - 1, dr), lambda b, t: (b, 0, 0)),
            pl.BlockSpec((sb, 1, dr), lambda b, t: (b, 0, 0)),
            _const_spec((d, dr)), _const_spec((1, dr)),
            _const_spec((d, dr)), _const_spec((1, dr)),
            _const_spec((LRU_CONV, dr)), _const_spec((1, dr)),
            _const_spec((LRU_BLOCKS, bw, 2 * bw)), _const_spec((1, dr)), _const_spec((1, dr)),
            _const_spec((1, dr)),
            _const_spec((dr, d)), _const_spec((1, d)),
        ],
        out_specs=[
            pl.BlockSpec((sb, tt, d), lambda b, t: (b, t, 0)),
            pl.BlockSpec((sb, LRU_CONV - 1, dr), lambda b, t: (b, 0, 0)),
            pl.BlockSpec((sb, 1, dr), lambda b, t: (b, 0, 0)),
        ],
        out_shape=[
            jax.ShapeDtypeStruct((bn, T, d), F32),
            jax.ShapeDtypeStruct((bn, LRU_CONV - 1, dr), F32),
            jax.ShapeDtypeStruct((bn, 1, dr), F32),
        ],
        scratch_shapes=[
            pltpu.VMEM((sb, tt + 8, dr), F32),
            pltpu.VMEM((sb * tt, dr), F32),
            pltpu.VMEM((sb * tt, dr), F32),
            pltpu.VMEM((sb * tt, dr), F32),
            pltpu.VMEM((sb * tt, dr), F32),
            pltpu.VMEM((sb, 8, dr), F32),
        ],
        compiler_params=_params(("arbitrary", "arbitrary")),
        name="lru",
    )(x, mods, ng, cst, hst.reshape(bn, 1, dr), p["wx"], row(p["bx"]), p["wy"], row(p["by"]),
      p["cw"], row(p["cb"]), p["wai"], row(p["ba"]), row(p["bi"]), row(p["lam"]),
      p["wo"], row(p["bo"]))
    return out, cst_new, hst_new.reshape(bn, dr)


def _ssd_kernel(x_ref, mod_ref, ng_ref, cst_ref, sst_ref, wz_ref, wxbc_ref, wdt_ref, cw_ref, cb_ref,
                dtb_ref, alog_ref, dsk_ref, sng_ref, wout_ref, e2_ref,
                o_ref, cst_out_ref, sst_out_ref, buf, xbc_s, cum_s, ce_s, dte_s, y_s, ST, *, sb, tt):
    t = pl.program_id(1)
    nt = pl.num_programs(1)
    d = x_ref.shape[-1]
    di = wz_ref.shape[-1]
    cdim = wxbc_ref.shape[-1]
    w = SSD_CONV
    L = SSD_CHUNK
    P = SSD_HEAD_DIM
    N = SSD_STATE
    G = SSD_GROUPS
    hpg = di // P // G
    rows = sb * tt

    @pl.when(t == 0)
    def _():
        buf[:, 0:8, :] = jnp.zeros((sb, 8, cdim), F32)
        buf[:, 8 - (w - 1):8, :] = cst_ref[...]
        for s in range(sb):
            ST[s] = sst_ref[s].T

    x = x_ref[...]
    m = mod_ref[...]
    h = _rms(x, ng_ref[0:1, :]) * (1.0 + m[:, 1:2, :]) + m[:, 0:1, :]
    hb = h.reshape(rows, d).astype(BF16)
    pre = _dot(hb, wxbc_ref[...]).reshape(sb, tt, cdim)
    buf[:, 8:8 + tt, :] = pre
    cv = _conv_from_buf(buf, pre, cw_ref, cb_ref, w, tt).reshape(rows, cdim)
    cst_out_ref[...] = buf[:, 8 + tt - (w - 1):8 + tt, :]
    buf[:, 0:8, :] = buf[:, tt:tt + 8, :]
    xbc_s[...] = cv * jax.nn.sigmoid(cv)

    dt = jax.nn.softplus(_dot(hb, wdt_ref[...]) + dtb_ref[...])
    a_neg = -jnp.exp(alog_ref[...])
    ri = lax.broadcasted_iota(jnp.int32, (rows, rows), 0)
    ci = lax.broadcasted_iota(jnp.int32, (rows, rows), 1)
    tri = jnp.where((ri >= ci) & (ri // L == ci // L), 1.0, 0.0).astype(BF16)
    cum = _dot_exact_lhs(tri, dt * a_neg)
    cum_s[...] = cum

    def expand(v):
        hi = v.astype(BF16)
        mid = (v - hi.astype(F32)).astype(BF16)
        return _dot(jnp.concatenate([hi, mid], axis=1), e2_ref[...])

    ce_s[...] = expand(cum)
    dte_s[...] = expand(dt)

    nck = tt // L
    lane2 = lax.broadcasted_iota(jnp.int32, (L, LANES), 1)
    low_b = lane2 < P
    tril2 = lax.broadcasted_iota(jnp.int32, (L, LANES), 0) >= (lane2 & (P - 1))
    gw = hpg * P

    def chunk(idx, carry):
        s = idx // nck
        r0 = pl.multiple_of(idx * L, L)
        rws = pl.ds(r0, L)
        cum_t = cum_s[rws, :].T
        for g in range(G):
            Bm = xbc_s[rws, di + g * N:di + (g + 1) * N]
            Cm = xbc_s[rws, di + G * N + g * N:di + G * N + (g + 1) * N]
            Bb = Bm.astype(BF16)
            Cb = Cm.astype(BF16)
            cb2 = _dot_nt(Cb, jnp.concatenate([Bb, Bb], axis=0))
            st_g = ST[s, :, g * gw:(g + 1) * gw]
            yo_g = _dot(Cb, st_g.astype(BF16))
            xws = []
            for pr in range(hpg // 2):
                j = g * (hpg // 2) + pr
                sl = slice(j * LANES, (j + 1) * LANES)
                ce2 = ce_s[rws, sl]
                xdt2 = xbc_s[rws, sl] * dte_s[rws, sl]
                crow2 = jnp.concatenate([cum_t[2 * j:2 * j + 1, :], cum_t[2 * j + 1:2 * j + 2, :]], axis=1)
                m2 = (cb2 * jnp.exp(jnp.where(tril2, ce2 - crow2, -jnp.inf))).astype(BF16)
                xb = xdt2.astype(BF16)
                zb = jnp.zeros_like(xb)
                rhs = jnp.concatenate([jnp.where(low_b, xb, zb), jnp.where(low_b, zb, xb)], axis=0)
                yd2 = _dot(m2, rhs)
                y_s[rws, sl] = yd2 + yo_g[:, pr * LANES:(pr + 1) * LANES] * jnp.exp(ce2)
                xws.append((xdt2 * jnp.exp(ce2[L - 1:L, :] - ce2)).astype(BF16))
            xw_g = jnp.concatenate(xws, axis=1)
            dec_g = jnp.exp(ce_s[pl.ds(r0 + L - 1, 1), g * gw:(g + 1) * gw])
            ST[s, :, g * gw:(g + 1) * gw] = st_g * dec_g + _dot(Bm.T.astype(BF16), xw_g)
        return carry

    lax.fori_loop(0, sb * nck, chunk, 0)

    @pl.when(t == nt - 1)
    def _():
        for s in range(sb):
            sst_out_ref[s] = ST[s].T

    z = _dot(hb, wz_ref[...])
    y = y_s[...] + dsk_ref[...] * xbc_s[:, 0:di]
    y = _rms(y * (z * jax.nn.sigmoid(z)), sng_ref[...])
    yo = _dot(y.astype(BF16), wout_ref[...]).reshape(sb, tt, d)
    o_ref[...] = x + m[:, 2:3, :] * _rms(yo, ng_ref[1:2, :])


def _ssd_call(x, mods, ng, cst, sst, p, sb, tt):
    bn, T, d = x.shape
    di = p["wz"].shape[-1]
    cdim = p["wxbc"].shape[-1]
    nh = di // SSD_HEAD_DIM
    kern = functools.partial(_ssd_kernel, sb=sb, tt=tt)
    out, cst_new, sst_new = pl.pallas_call(
        kern,
        grid=(bn // sb, T // tt),
        in_specs=[
            pl.BlockSpec((sb, tt, d), lambda b, t: (b, t, 0)),
            pl.BlockSpec((sb, 6, d), lambda b, t: (b, 0, 0)),
            _const_spec((4, d)),
            pl.BlockSpec((sb, SSD_CONV - 1, cdim), lambda b, t: (b, 0, 0)),
            pl.BlockSpec((sb, di, SSD_STATE), lambda b, t: (b, 0, 0)),
            _const_spec((d, di)), _const_spec((d, cdim)), _const_spec((d, LANES)),
            _const_spec((SSD_CONV, cdim)), _const_spec((1, cdim)),
            _const_spec((1, LANES)), _const_spec((1, LANES)),
            _const_spec((1, di)), _const_spec((1, di)),
            _const_spec((di, d)), _const_spec((2 * LANES, di)),
        ],
        out_specs=[
            pl.BlockSpec((sb, tt, d), lambda b, t: (b, t, 0)),
            pl.BlockSpec((sb, SSD_CONV - 1, cdim), lambda b, t: (b, 0, 0)),
            pl.BlockSpec((sb, di, SSD_STATE), lambda b, t: (b, 0, 0)),
        ],
        out_shape=[
            jax.ShapeDtypeStruct((bn, T, d), F32),
            jax.ShapeDtypeStruct((bn, SSD_CONV - 1, cdim), F32),
            jax.ShapeDtypeStruct((bn, di, SSD_STATE), F32),
        ],
        scratch_shapes=[
            pltpu.VMEM((sb, tt + 8, cdim), F32),
            pltpu.VMEM((sb * tt, cdim), F32),
            pltpu.VMEM((sb * tt, LANES), F32),
            pltpu.VMEM((sb * tt, di), F32),
            pltpu.VMEM((sb * tt, di), F32),
            pltpu.VMEM((sb * tt, di), F32),
            pltpu.VMEM((sb, SSD_STATE, di), F32),
        ],
        compiler_params=_params(("arbitrary", "arbitrary")),
        name="ssd",
    )(x, mods, ng, cst, sst.reshape(bn, di, SSD_STATE), p["wz"], p["wxbc"], p["wdt"], p["cw"], p["cb"],
      p["dtb"], p["alog"], p["dsk"], p["sng"], p["wout"], p["e2"])
    return out, cst_new, sst_new.reshape(bn, nh, SSD_HEAD_DIM, SSD_STATE)


def _fox_proj_kernel(*refs, sb, tt, past):
    if past:
        (x_ref, mod_ref, ng_ref, lfp_ref, wqkv_ref, wf_ref, bf_ref,
         q_ref, k_ref, v_ref, kb_ref, vb_ref, lf_ref, fq_ref, fp_ref, fcar) = refs
    else:
        (x_ref, mod_ref, ng_ref, wqkv_ref, wf_ref, bf_ref,
         q_ref, k_ref, v_ref, kb_ref, vb_ref, lf_ref, fq_ref, fcar) = refs
    t = pl.program_id(1)
    d = x_ref.shape[-1]
    nh = lf_ref.shape[-1]
    rows = sb * tt
    cblk = min(tt, 256)
    trt = jnp.where(_tril(cblk), 1.0, 0.0).astype(BF16)

    def cumsum_rows(get_rows, n, car):
        outs = []
        for r in range(0, n, cblk):
            c = _dot_exact_lhs(trt, get_rows(r, cblk)) + car
            outs.append(c)
            car = c[cblk - 1:cblk, :]
        return outs, car

    @pl.when(t == 0)
    def _():
        if past:
            plen = lfp_ref.shape[1]
            for s in range(sb):
                outs, car = cumsum_rows(lambda r, n, s=s: lfp_ref[s, r:r + n, :], plen, jnp.zeros((1, LANES), F32))
                for bi, c in enumerate(outs):
                    fp_ref[s, bi * cblk:(bi + 1) * cblk, :] = c[:, 0:nh] * LOG2E
                fcar[s] = jnp.broadcast_to(car, (8, LANES))
        else:
            fcar[...] = jnp.zeros((sb, 8, LANES), F32)

    x = x_ref[...]
    m = mod_ref[...]
    h = _rms(x, ng_ref[0:1, :]) * (1.0 + m[:, 1:2, :]) + m[:, 0:1, :]
    hb = h.reshape(rows, d).astype(BF16)
    qkv = _dot(hb, wqkv_ref[...])
    q = qkv[:, 0:d] * (FOX_HEAD_DIM ** -0.5 * LOG2E)
    k = qkv[:, d:2 * d]
    v = qkv[:, 2 * d:3 * d]
    q_ref[...] = q.astype(BF16).reshape(sb, tt, d)
    k_ref[...] = k.reshape(sb, tt, d)
    v_ref[...] = v.reshape(sb, tt, d)
    kb_ref[...] = k.astype(BF16).reshape(sb, tt, d)
    vb_ref[...] = v.astype(BF16).reshape(sb, tt, d)
    lf = jax.nn.log_sigmoid(_dot(hb, wf_ref[...]) + bf_ref[...])
    lf_ref[...] = lf[:, 0:nh].reshape(sb, tt, nh)
    for s in range(sb):
        outs, car = cumsum_rows(lambda r, n, s=s: lf[s * tt + r:s * tt + r + n, :], tt, fcar[s, 0:1, :])
        for bi, c in enumerate(outs):
            fq_ref[s, bi * cblk:(bi + 1) * cblk, :] = c[:, 0:nh] * LOG2E
        fcar[s] = jnp.broadcast_to(car, (8, LANES))


def _fox_proj_call(x, mods, ng, lf_past, wqkv, wf, bf, sb, tt):
    bn, T, d = x.shape
    nh = FOX_HEADS
    past = lf_past is not None
    kern = functools.partial(_fox_proj_kernel, sb=sb, tt=tt, past=past)
    in_specs = [
        pl.BlockSpec((sb, tt, d), lambda b, t: (b, t, 0)),
        pl.BlockSpec((sb, 6, d), lambda b, t: (b, 0, 0)),
        _const_spec((4, d)),
    ]
    args = [x, mods, ng]
    if past:
        plen = lf_past.shape[1]
        in_specs.append(pl.BlockSpec((sb, plen, LANES), lambda b, t: (b, 0, 0)))
        args.append(jnp.pad(lf_past, ((0, 0), (0, 0), (0, LANES - nh))))
    in_specs += [_const_spec((d, 3 * d)), _const_spec((d, LANES)), _const_spec((1, LANES))]
    args += [wqkv, wf, jnp.pad(bf.reshape(1, nh), ((0, 0), (0, LANES - nh)))]
    tile = pl.BlockSpec((sb, tt, d), lambda b, t: (b, t, 0))
    small = pl.BlockSpec((sb, tt, nh), lambda b, t: (b, t, 0))
    out_specs = [tile, tile, tile, tile, tile, small, small]
    out_shape = [
        jax.ShapeDtypeStruct((bn, T, d), BF16),
        jax.ShapeDtypeStruct((bn, T, d), F32),
        jax.ShapeDtypeStruct((bn, T, d), F32),
        jax.ShapeDtypeStruct((bn, T, d), BF16),
        jax.ShapeDtypeStruct((bn, T, d), BF16),
        jax.ShapeDtypeStruct((bn, T, nh), F32),
        jax.ShapeDtypeStruct((bn, T, nh), F32),
    ]
    if past:
        out_specs.append(pl.BlockSpec((sb, plen, nh), lambda b, t: (b, 0, 0)))
        out_shape.append(jax.ShapeDtypeStruct((bn, plen, nh), F32))
    return pl.pallas_call(
        kern,
        grid=(bn // sb, T // tt),
        in_specs=in_specs,
        out_specs=out_specs,
        out_shape=out_shape,
        scratch_shapes=[pltpu.VMEM((sb, 8, LANES), F32)],
        compiler_params=_params(("arbitrary", "arbitrary")),
        name="fox_proj",
    )(*args)


def _fox_attn_kernel(x_ref, mod_ref, ng_ref, q_ref, k_ref, v_ref, fq_ref, fkt_ref, wo_ref,
                     o_ref, acc, lacc, m_s, fq_s, *, tq, tk, q_off, k_pad, rs):
    qi = pl.program_id(1)
    ki = pl.program_id(2)
    nk = pl.num_programs(2)
    d = x_ref.shape[-1]
    npair = d // LANES
    hd = FOX_HEAD_DIM
    q_lo = q_off + qi * tq
    k_lo = ki * tk
    nstrip = tq // rs

    @pl.when(ki == 0)
    def _():
        acc[...] = jnp.zeros(acc.shape, F32)
        lacc[...] = jnp.zeros(lacc.shape, F32)
        m_s[...] = jnp.full(m_s.shape, -1e30, F32)
        for hh in range(2 * npair):
            fq_s[hh] = jnp.broadcast_to(fq_ref[:, hh:hh + 1], (tq, LANES))

    low = lax.broadcasted_iota(jnp.int32, (tq, LANES), 1) < hd
    low_k = lax.broadcasted_iota(jnp.int32, (tk, LANES), 1) < hd

    def step(masked):
        ncb = tk // LANES
        if masked:
            col = lax.broadcasted_iota(jnp.int32, (rs, LANES), 1)
            dcol = col - lax.broadcasted_iota(jnp.int32, (rs, LANES), 0)

        def block_kind(i, c):
            if not masked:
                return 2
            first_vis, last_vis = q_off + i * rs, q_off + i * rs + rs - 1
            if c * LANES > last_vis or (c + 1) * LANES <= k_pad:
                return 0
            if c * LANES + LANES - 1 <= first_vis and c * LANES >= k_pad:
                return 2
            return 1

        def logits(s, fk, i, c):
            t = s[i * rs:(i + 1) * rs, c * LANES:(c + 1) * LANES] - fk[:, c * LANES:(c + 1) * LANES]
            if block_kind(i, c) == 1:
                ok = dcol <= (q_off + i * rs - c * LANES)
                if c * LANES < k_pad:
                    ok = ok & (col >= k_pad - c * LANES)
                t = jnp.where(ok, t, -jnp.inf)
            return t

        for j in range(npair):
            q2 = q_ref[:, j * LANES:(j + 1) * LANES]
            k2 = k_ref[:, j * LANES:(j + 1) * LANES]
            v2 = v_ref[:, j * LANES:(j + 1) * LANES]
            ones = jnp.ones_like(v2)
            vaug = (jnp.where(low_k, v2, ones), jnp.where(low_k, ones, v2))
            outs, alphas = [], []
            for half in range(2):
                hh = 2 * j + half
                qh = jnp.where(low if half == 0 else ~low, q2, jnp.zeros_like(q2))
                s = _dot_nt(qh, k2)
                fk = fkt_ref[hh:hh + 1, :]
                fq = fq_s[hh]
                m_old = m_s[hh]

                pms = []
                for i in range(nstrip):
                    pm = None
                    for c in range(ncb):
                        if block_kind(i, c) > 0:
                            t = logits(s, fk, i, c)
                            pm = t if pm is None else jnp.maximum(pm, t)
                    pms.append(pm)
                mx = jnp.max(jnp.concatenate(pms, axis=0), axis=-1, keepdims=True)
                m_new = jnp.maximum(m_old, mx + fq)
                alpha = jnp.exp2(m_old - m_new)
                c_b = m_new - fq
                prow = []
                for i in range(nstrip):
                    cbs = c_b[i * rs:(i + 1) * rs]
                    pieces = []
                    for c in range(ncb):
                        if block_kind(i, c) > 0:
                            pieces.append(jnp.exp2(logits(s, fk, i, c) - cbs).astype(BF16))
                        else:
                            pieces.append(jnp.zeros((rs, LANES), BF16))
                    prow.append(jnp.concatenate(pieces, axis=1))
                m_s[hh] = m_new
                outs.append(_dot(jnp.concatenate(prow, axis=0), vaug[half]))
                alphas.append(alpha)
            sl = slice(j * LANES, (j + 1) * LANES)
            acc[:, sl] = jnp.where(low, alphas[0], alphas[1]) * acc[:, sl] + jnp.where(low, outs[0], outs[1])
            lacc[:, sl] = jnp.where(low, alphas[1], alphas[0]) * lacc[:, sl] + jnp.where(low, outs[1], outs[0])

    needed = k_lo <= q_lo + tq - 1
    need_mask = (k_lo + tk - 1 > q_lo) | (k_lo < k_pad)

    @pl.when(needed & need_mask)
    def _():
        step(True)

    @pl.when(needed & jnp.logical_not(need_mask))
    def _():
        step(False)

    @pl.when(ki == nk - 1)
    def _():
        cols = []
        for j in range(npair):
            l2 = pltpu.roll(lacc[:, j * LANES:(j + 1) * LANES], hd, 1)
            cols.append((acc[:, j * LANES:(j + 1) * LANES] / l2).astype(BF16))
        o = jnp.concatenate(cols, axis=1)
        y = _dot(o, wo_ref[...])
        m = mod_ref[...]
        o_ref[...] = x_ref[...] + m[2:3, :] * _rms(y, ng_ref[1:2, :])


def _fox_attn_call(x, mods, ng, q, kb, vb, fq, fkt, wo, tq, tk, q_off, k_pad):
    bn, T, d = x.shape
    tkeys = kb.shape[1]
    nh = FOX_HEADS
    nq, nk = T // tq, tkeys // tk
    assert (nq == 1 and nk == 1) or (q_off == 0 and k_pad == 0 and tq == tk)
    rs = min(32, tq)
    kern = functools.partial(_fox_attn_kernel, tq=tq, tk=tk, q_off=q_off, k_pad=k_pad, rs=rs)

    def kmap(b, i, j):
        last = (q_off + i * tq + tq - 1) // tk
        return (b, jnp.minimum(j, last), 0)

    def fkmap(b, i, j):
        last = (q_off + i * tq + tq - 1) // tk
        return (b, 0, jnp.minimum(j, last))

    return pl.pallas_call(
        kern,
        grid=(bn, T // tq, tkeys // tk),
        in_specs=[
            pl.BlockSpec((None, tq, d), lambda b, i, j: (b, i, 0)),
            pl.BlockSpec((None, 6, d), lambda b, i, j: (b, 0, 0)),
            _const_spec((4, d)),
            pl.BlockSpec((None, tq, d), lambda b, i, j: (b, i, 0)),
            pl.BlockSpec((None, tk, d), kmap),
            pl.BlockSpec((None, tk, d), kmap),
            pl.BlockSpec((None, tq, nh), lambda b, i, j: (b, i, 0)),
            pl.BlockSpec((None, nh, tk), fkmap),
            _const_spec((d, d)),
        ],
        out_specs=pl.BlockSpec((None, tq, d), lambda b, i, j: (b, i, 0)),
        out_shape=jax.ShapeDtypeStruct((bn, T, d), F32),
        scratch_shapes=[
            pltpu.VMEM((tq, d), F32),
            pltpu.VMEM((tq, d), F32),
            pltpu.VMEM((nh, tq, LANES), F32),
            pltpu.VMEM((nh, tq, LANES), F32),
        ],
        compiler_params=_params(("arbitrary", "arbitrary", "arbitrary")),
        name="fox_attn",
    )(x, mods, ng, q, kb, vb, fq, fkt, wo)


def _prep_weights(W):
    bf = lambda a: a.astype(BF16)
    di = W["ssd_w_out"].shape[0]
    nh = di // SSD_HEAD_DIM
    cdim = W["ssd_conv_w"].shape[-1]
    d = W["ssd_w_in"].shape[0]
    lru = []
    for j in range(W["lru_w_x"].shape[0]):
        lru.append(dict(
            wx=bf(W["lru_w_x"][j]), bx=W["lru_b_x"][j], wy=bf(W["lru_w_y"][j]), by=W["lru_b_y"][j],
            cw=W["lru_conv_w"][j], cb=W["lru_conv_b"][j],
            wai=bf(0.5 * jnp.concatenate([W["lru_w_a"][j], W["lru_w_i"][j]], axis=-1)),
            ba=0.5 * W["lru_b_a"][j], bi=0.5 * W["lru_b_i"][j], lam=W["lru_lambda"][j],
            wo=bf(W["lru_w_o"][j]), bo=W["lru_b_o"][j]))
    w_in = W["ssd_w_in"]
    pad_l = lambda v: jnp.pad(v.reshape(1, -1), ((0, 0), (0, LANES - v.shape[-1])))
    onehot = (jnp.arange(LANES)[:, None] == jnp.arange(di)[None, :] // SSD_HEAD_DIM).astype(BF16)
    ssd = dict(
        e2=jnp.concatenate([onehot, onehot], axis=0),
        wz=bf(w_in[:, :di]), wxbc=bf(w_in[:, di:di + cdim]),
        wdt=bf(jnp.pad(w_in[:, di + cdim:], ((0, 0), (0, LANES - nh)))),
        cw=W["ssd_conv_w"], cb=W["ssd_conv_b"].reshape(1, cdim),
        dtb=pad_l(W["ssd_dt_bias"]), alog=pad_l(W["ssd_a_log"]),
        dsk=jnp.repeat(W["ssd_d"], SSD_HEAD_DIM).reshape(1, di), sng=W["ssd_norm_g"].reshape(1, di),
        wout=bf(W["ssd_w_out"]))
    fox = dict(
        wqkv=bf(W["fox_w_qkv"]),
        wf=bf(jnp.pad(W["fox_w_f"], ((0, 0), (0, LANES - W["fox_w_f"].shape[-1])))),
        bf=W["fox_b_f"], wo=bf(W["fox_w_o"]))
    ffn = [dict(wg=bf(W["ffn_w_gate"][l]), wu=bf(W["ffn_w_up"][l]), cw=W["ffn_conv_w"][l],
                cb=W["ffn_conv_b"][l], wd=bf(W["ffn_w_down"][l])) for l in range(W["ffn_w_gate"].shape[0])]
    return dict(lru=lru, ssd=ssd, fox=fox, ffn=ffn, norm_g=W["norm_g"])


def _run_trunk(x, mods, prev, P, cfg):
    bn, T, d = x.shape
    sb, tt, tt_ffn, tq, tk = cfg["sb"], cfg["tt"], cfg["tt_ffn"], cfg["tq"], cfg["tk"]
    depth = mods.shape[0]
    new = {}
    ffn_bufs = []
    for layer in range(depth):
        ng = P["norm_g"][layer]
        ml = mods[layer]
        kind = layer % 3
        tag = "l%d" % layer
        if kind == 0:
            x, cn, hn = _lru_call(x, ml, ng, prev[tag + "_conv"], prev[tag + "_h"], P["lru"][layer // 3], sb, tt)
            new[tag + "_conv"], new[tag + "_h"] = cn, hn
        elif kind == 1:
            x, cn, sn = _ssd_call(x, ml, ng, prev[tag + "_conv"], prev[tag + "_ssm"], P["ssd"], cfg["sb_ssd"], tt)
            new[tag + "_conv"], new[tag + "_ssm"] = cn, sn
        else:
            fx = P["fox"]
            kp, vp, lp = prev.get(tag + "_k"), prev.get(tag + "_v"), prev.get(tag + "_logf")
            res = _fox_proj_call(x, ml, ng, lp, fx["wqkv"], fx["wf"], fx["bf"], sb, tt)
            q, k, v, kb, vb, lf, fq = res[:7]
            if kp is None:
                fk, q_off, k_pad = fq, 0, 0
            else:
                plen = kp.shape[1]
                tk = -(-(plen + T) // LANES) * LANES
                k_pad = tk - (plen + T)
                zk = jnp.zeros((bn, k_pad, d), BF16)
                kb = jnp.concatenate([zk, kp.reshape(bn, plen, d).astype(BF16), kb], axis=1)
                vb = jnp.concatenate([zk, vp.reshape(bn, plen, d).astype(BF16), vb], axis=1)
                fk = jnp.concatenate([jnp.zeros((bn, k_pad, FOX_HEADS), F32), res[7], fq], axis=1)
                q_off = k_pad + plen
            fkt = jnp.swapaxes(fk, 1, 2)
            x = _fox_attn_call(x, ml, ng, q, kb, vb, fq, fkt, fx["wo"], tq, tk, q_off, k_pad)
            new[tag + "_k"] = k.reshape(bn, T, FOX_HEADS, FOX_HEAD_DIM)
            new[tag + "_v"] = v.reshape(bn, T, FOX_HEADS, FOX_HEAD_DIM)
            new[tag + "_logf"] = lf
        f = P["ffn"][layer]
        x, buf = _ffn_call(x, ml, ng, prev["ffn_conv"][layer], f["wg"], f["wu"], f["cw"], f["cb"], f["wd"],
                           sb, tt_ffn)
        ffn_bufs.append(buf)
    new["ffn_conv"] = jnp.stack(ffn_bufs)
    return x, new


def _fresh_state(bn, W):
    dr = W["lru_w_x"].shape[-1]
    cdim = W["ssd_conv_w"].shape[-1]
    di = W["ssd_w_out"].shape[0]
    f = W["ffn_w_gate"].shape[-1]
    depth = W["ffn_w_gate"].shape[0]
    st = dict(ffn_conv=jnp.zeros((depth, bn, FFN_CONV - 1, f), F32))
    for layer in range(depth):
        tag = "l%d" % layer
        if layer % 3 == 0:
            st[tag + "_conv"] = jnp.zeros((bn, LRU_CONV - 1, dr), F32)
            st[tag + "_h"] = jnp.zeros((bn, dr), F32)
        elif layer % 3 == 1:
            st[tag + "_conv"] = jnp.zeros((bn, SSD_CONV - 1, cdim), F32)
            st[tag + "_ssm"] = jnp.zeros((bn, di // SSD_HEAD_DIM, SSD_HEAD_DIM, SSD_STATE), F32)
    return st


def _tile_cfg(bn, T):
    if T >= 256:
        return dict(sb=1, sb_ssd=1, tt=256, tt_ffn=256, tq=512 if T % 512 == 0 else 256,
                    tk=512 if T % 512 == 0 else 256)
    sb = 4 if bn % 4 == 0 else 1
    return dict(sb=sb, sb_ssd=2 if bn % 2 == 0 else 1, tt=T, tt_ffn=T, tq=T, tk=128)


def kernel(x_prompt, x_sample, c_prompt, c_sample, state_l0_conv, state_l0_h, state_l1_conv, state_l1_ssm, cache_l2_k, cache_l2_v, cache_l2_logf, state_l3_conv, state_l3_h, state_ffn_conv, ada_w, ada_b, norm_g, lru_w_x, lru_b_x, lru_w_y, lru_b_y, lru_conv_w, lru_conv_b, lru_w_a, lru_b_a, lru_w_i, lru_b_i, lru_lambda, lru_w_o, lru_b_o, ssd_w_in, ssd_conv_w, ssd_conv_b, ssd_dt_bias, ssd_a_log, ssd_d, ssd_norm_g, ssd_w_out, fox_w_qkv, fox_w_f, fox_b_f, fox_w_o, ffn_w_gate, ffn_w_up, ffn_conv_w, ffn_conv_b, ffn_w_down):
    W = dict(ada_w=ada_w, ada_b=ada_b, norm_g=norm_g,
             lru_w_x=lru_w_x, lru_b_x=lru_b_x, lru_w_y=lru_w_y, lru_b_y=lru_b_y,
             lru_conv_w=lru_conv_w, lru_conv_b=lru_conv_b, lru_w_a=lru_w_a, lru_b_a=lru_b_a,
             lru_w_i=lru_w_i, lru_b_i=lru_b_i, lru_lambda=lru_lambda, lru_w_o=lru_w_o, lru_b_o=lru_b_o,
             ssd_w_in=ssd_w_in, ssd_conv_w=ssd_conv_w, ssd_conv_b=ssd_conv_b, ssd_dt_bias=ssd_dt_bias,
             ssd_a_log=ssd_a_log, ssd_d=ssd_d, ssd_norm_g=ssd_norm_g, ssd_w_out=ssd_w_out,
             fox_w_qkv=fox_w_qkv, fox_w_f=fox_w_f, fox_b_f=fox_b_f, fox_w_o=fox_w_o,
             ffn_w_gate=ffn_w_gate, ffn_w_up=ffn_w_up, ffn_conv_w=ffn_conv_w, ffn_conv_b=ffn_conv_b,
             ffn_w_down=ffn_w_down)
    P = _prep_weights(W)
    bp, bs = x_prompt.shape[0], x_sample.shape[0]
    d = x_prompt.shape[-1]
    depth = ada_w.shape[0]
    mods = _ada_call(jnp.concatenate([c_prompt, c_sample], axis=0), ada_w, ada_b)
    mods = mods.reshape(depth, bp + bs, 6, d)

    y_prompt, p = _run_trunk(x_prompt, mods[:, :bp], _fresh_state(bp, W), P, _tile_cfg(bp, x_prompt.shape[1]))
    prev = dict(l0_conv=state_l0_conv, l0_h=state_l0_h, l1_conv=state_l1_conv, l1_ssm=state_l1_ssm,
                l2_k=cache_l2_k, l2_v=cache_l2_v, l2_logf=cache_l2_logf,
                l3_conv=state_l3_conv, l3_h=state_l3_h, ffn_conv=state_ffn_conv)
    y_sample, s = _run_trunk(x_sample, mods[:, bp:], prev, P, _tile_cfg(bs, x_sample.shape[1]))
    return (y_prompt, y_sample,
            p['l0_conv'], p['l0_h'], p['l1_conv'], p['l1_ssm'], p['l2_k'], p['l2_v'], p['l2_logf'],
            p['l3_conv'], p['l3_h'], p['ffn_conv'],
            s['l0_conv'], s['l0_h'], s['l1_conv'], s['l1_ssm'], s['l2_k'], s['l2_v'], s['l2_logf'],
            s['l3_conv'], s['l3_h'], s['ffn_conv'])
```

```python
import functools

import jax
import jax.numpy as jnp
from jax import lax
from jax.experimental import pallas as pl
from jax.experimental.pallas import tpu as pltpu

F32 = jnp.float32
BF16 = jnp.bfloat16

EPS = 1e-6
LOG2E = 1.4426950408889634
LRU_C = 8.0
LRU_BLOCKS = 8
LRU_CONV = 4
SSD_HEAD_DIM = 64
SSD_GROUPS = 4
SSD_STATE = 128
SSD_CONV = 4
SSD_CHUNK = 64
FOX_HEADS = 16
FOX_HEAD_DIM = 64
FFN_CONV = 3

LANES = 128
SUBLANES = 8
VMEM_LIMIT = 56 * 1024 * 1024


def _const_spec(shape):
    nd = len(shape)
    return pl.BlockSpec(shape, lambda *_: (0,) * nd, pipeline_mode=pl.Buffered(1))


def _params(sem):
    return pltpu.CompilerParams(dimension_semantics=sem, vmem_limit_bytes=VMEM_LIMIT)


def _rms(x, g):
    return x * lax.rsqrt(jnp.mean(x * x, axis=-1, keepdims=True) + EPS) * g


def _dot(a, b):
    return jnp.dot(a, b, preferred_element_type=F32)


def _dot_nt(a, b):
    return lax.dot_general(a, b, (((1,), (1,)), ((), ())), preferred_element_type=F32)


def _dot_tn(a, b):
    return lax.dot_general(a, b, (((0,), (0,)), ((), ())), preferred_element_type=F32)


def _split3(x):
    hi = x.astype(BF16)
    r1 = x - hi.astype(F32)
    mid = r1.astype(BF16)
    lo = (r1 - mid.astype(F32)).astype(BF16)
    return hi, mid, lo


def _dot_exact_lhs(m, x):
    hi, mid, lo = _split3(x)
    return _dot(m, hi) + (_dot(m, mid) + _dot(m, lo))


def _expm1_nonpos(x):
    u = jnp.exp(x)
    small = (u - 1.0) * x / jnp.log(u)
    return jnp.where(x < -1.0, u - 1.0, jnp.where(u == 1.0, x, small))


def _gelu_tanh(x):
    c0 = 0.7978845608028654
    hx = 0.5 * x
    return hx + hx * jnp.tanh(x * (c0 + (c0 * 0.044715) * (x * x)))


def _tril(n):
    r = lax.broadcasted_iota(jnp.int32, (n, n), 0)
    c = lax.broadcasted_iota(jnp.int32, (n, n), 1)
    return r >= c


def _ada_kernel(c_ref, w_ref, b_ref, o_ref):
    c = c_ref[...]
    s = (c * jax.nn.sigmoid(c)).astype(BF16)
    o_ref[...] = _dot(s, w_ref[...].astype(BF16)) + b_ref[...]


def _ada_call(c_all, ada_w, ada_b):
    depth, d, n = ada_w.shape
    bn = c_all.shape[0]
    tn = 1536 if n % 1536 == 0 else n
    return pl.pallas_call(
        _ada_kernel,
        grid=(depth, n // tn),
        in_specs=[
            pl.BlockSpec((bn, d), lambda l, j: (0, 0)),
            pl.BlockSpec((None, d, tn), lambda l, j: (l, 0, j)),
            pl.BlockSpec((None, 1, tn), lambda l, j: (l, 0, j)),
        ],
        out_specs=pl.BlockSpec((None, bn, tn), lambda l, j: (l, 0, j)),
        out_shape=jax.ShapeDtypeStruct((depth, bn, n), F32),
        compiler_params=_params(("arbitrary", "arbitrary")),
        name="ada",
    )(c_all, ada_w, ada_b.reshape(depth, 1, n))


def _conv_from_buf(buf, cur, w_ref, b_ref, width, tt):
    y = b_ref[...] + w_ref[width - 1:width, :] * cur
    for j in range(1, width):
        y = y + w_ref[width - 1 - j:width - j, :] * buf[:, 8 - j:8 - j + tt, :]
    return y


def _ffn_kernel(x_ref, mod_ref, ng_ref, st_ref, wg_ref, wu_ref, cw_ref, cb_ref, wd_ref,
                o_ref, st_out_ref, buf, *, sb, tt):
    t = pl.program_id(1)
    d = x_ref.shape[-1]
    f = wg_ref.shape[-1]
    w = FFN_CONV

    @pl.when(t == 0)
    def _():
        buf[:, 0:8, :] = jnp.zeros((sb, 8, f), F32)
        buf[:, 8 - (w - 1):8, :] = st_ref[...]

    x = x_ref[...]
    m = mod_ref[...]
    h = _rms(x, ng_ref[2:3, :]) * (1.0 + m[:, 4:5, :]) + m[:, 3:4, :]
    hb = h.reshape(sb * tt, d).astype(BF16)
    gp = _dot(hb, wg_ref[...]).reshape(sb, tt, f)
    up = _dot(hb, wu_ref[...])
    buf[:, 8:8 + tt, :] = gp
    g = _conv_from_buf(buf, gp, cw_ref, cb_ref, w, tt)
    st_out_ref[...] = buf[:, 8 + tt - (w - 1):8 + tt, :]
    buf[:, 0:8, :] = buf[:, tt:tt + 8, :]
    act = (_gelu_tanh(g).reshape(sb * tt, f) * up).astype(BF16)
    y = _dot(act, wd_ref[...]).reshape(sb, tt, d)
    o_ref[...] = x + m[:, 5:6, :] * _rms(y, ng_ref[3:4, :])


def _ffn_call(x, mods, ng, st, wg, wu, cw, cb, wd, sb, tt):
    bn, T, d = x.shape
    f = wg.shape[-1]
    kern = functools.partial(_ffn_kernel, sb=sb, tt=tt)
    return pl.pallas_call(
        kern,
        grid=(bn // sb, T // tt),
        in_specs=[
            pl.BlockSpec((sb, tt, d), lambda b, t: (b, t, 0)),
            pl.BlockSpec((sb, 6, d), lambda b, t: (b, 0, 0)),
            _const_spec((4, d)),
            pl.BlockSpec((sb, FFN_CONV - 1, f), lambda b, t: (b, 0, 0)),
            _const_spec((d, f)),
            _const_spec((d, f)),
            _const_spec((FFN_CONV, f)),
            _const_spec((1, f)),
            _const_spec((f, d)),
        ],
        out_specs=[
            pl.BlockSpec((sb, tt, d), lambda b, t: (b, t, 0)),
            pl.BlockSpec((sb, FFN_CONV - 1, f), lambda b, t: (b, 0, 0)),
        ],
        out_shape=[
            jax.ShapeDtypeStruct((bn, T, d), F32),
            jax.ShapeDtypeStruct((bn, FFN_CONV - 1, f), F32),
        ],
        scratch_shapes=[pltpu.VMEM((sb, tt + 8, f), F32)],
        compiler_params=_params(("arbitrary", "arbitrary")),
        name="ffn",
    )(x, mods, ng, st, wg, wu, cw, cb.reshape(1, f), wd)


def _lru_kernel(x_ref, mod_ref, ng_ref, cst_ref, hst_ref, wx_ref, bx_ref, wy_ref, by_ref,
                cw_ref, cb_ref, wai_ref, ba_ref, bi_ref, lam_ref, wo_ref, bo_ref,
                o_ref, cst_out_ref, hst_out_ref, buf, za_s, zi_s, u_s, g_s, hcar, *, sb, tt):
    t = pl.program_id(1)
    d = x_ref.shape[-1]
    dr = wx_ref.shape[-1]
    bw = dr // LRU_BLOCKS
    w = LRU_CONV
    rows = sb * tt

    @pl.when(t == 0)
    def _():
        buf[:, 0:8, :] = jnp.zeros((sb, 8, dr), F32)
        buf[:, 8 - (w - 1):8, :] = cst_ref[...]
        hcar[...] = jnp.broadcast_to(hst_ref[...], (sb, 8, dr))

    x = x_ref[...]
    m = mod_ref[...]
    h = _rms(x, ng_ref[0:1, :]) * (1.0 + m[:, 1:2, :]) + m[:, 0:1, :]
    hb = h.reshape(rows, d).astype(BF16)
    xx = (_dot(hb, wx_ref[...]) + bx_ref[...]).reshape(sb, tt, dr)
    buf[:, 8:8 + tt, :] = xx
    u = _conv_from_buf(buf, xx, cw_ref, cb_ref, w, tt).reshape(rows, dr)
    cst_out_ref[...] = buf[:, 8 + tt - (w - 1):8 + tt, :]
    buf[:, 0:8, :] = buf[:, tt:tt + 8, :]

    u_s[...] = u
    ub = u.astype(BF16)
    for n in range(LRU_BLOCKS):
        z = _dot(ub[:, n * bw:(n + 1) * bw], wai_ref[n])
        za_s[:, n * bw:(n + 1) * bw] = z[:, :bw]
        zi_s[:, n * bw:(n + 1) * bw] = z[:, bw:]
    g_s[...] = _dot(hb, wy_ref[...]) + by_ref[...]

    lc = 256
    row8 = lax.broadcasted_iota(jnp.int32, (8, lc), 0)
    khalf = (-0.5 * LRU_C) * jax.nn.softplus(-lam_ref[...])
    for s in range(sb):
        def body(gi, car, s=s):
            r0 = pl.multiple_of(s * tt + gi * 8, 8)
            new_car = []
            for c in range(dr // lc):
                ls = slice(c * lc, (c + 1) * lc)
                kh = khalf[:, ls]
                log_a = kh + kh * jnp.tanh(za_s[pl.ds(r0, 8), ls] + ba_ref[:, ls])
                hu = 0.5 * u_s[pl.ds(r0, 8), ls]
                iu = hu + hu * jnp.tanh(zi_s[pl.ds(r0, 8), ls] + bi_ref[:, ls])
                A = jnp.exp(log_a)
                B = jnp.sqrt(jnp.tanh(-log_a) * (1.0 + A * A)) * iu
                for sh in (1, 2, 4):
                    As = pltpu.roll(A, sh, 0)
                    Bs = pltpu.roll(B, sh, 0)
                    msk = row8 >= sh
                    B = jnp.where(msk, A * Bs + B, B)
                    A = jnp.where(msk, A * As, A)
                H = A * car[c] + B
                g_s[pl.ds(r0, 8), ls] = H * _gelu_tanh(g_s[pl.ds(r0, 8), ls])
                new_car.append(jnp.broadcast_to(H[7:8, :], (8, lc)))
            return tuple(new_car)

        car0 = tuple(hcar[s, :, c * lc:(c + 1) * lc] for c in range(dr // lc))
        car = lax.fori_loop(0, tt // 8, body, car0, unroll=2)
        for c in range(dr // lc):
            hcar[s, :, c * lc:(c + 1) * lc] = car[c]
    hst_out_ref[...] = hcar[:, 7:8, :]

    y = (_dot(g_s[...].astype(BF16), wo_ref[...]) + bo_ref[...]).reshape(sb, tt, d)
    o_ref[...] = x + m[:, 2:3, :] * _rms(y, ng_ref[1:2, :])


def _lru_call(x, mods, ng, cst, hst, p, sb, tt):
    bn, T, d = x.shape
    dr = p["wx"].shape[-1]
    bw = dr // LRU_BLOCKS
    kern = functools.partial(_lru_kernel, sb=sb, tt=tt)
    row = lambda v: v.reshape(1, -1)
    out, cst_new, hst_new = pl.pallas_call(
        kern,
        grid=(bn // sb, T // tt),
        in_specs=[
            pl.BlockSpec((sb, tt, d), lambda b, t: (b, t, 0)),
            pl.BlockSpec((sb, 6, d), lambda b, t: (b, 0, 0)),
            _const_spec((4, d)),
            pl.BlockSpec((sb, LRU_CONV - 1, dr), lambda b, t: (b, 0, 0)),
            pl.BlockSpec((sb, 1, dr), lambda b, t: (b, 0, 0)),
            _const_spec((d, dr)), _const_spec((1, dr)),
            _const_spec((d, dr)), _const_spec((1, dr)),
            _const_spec((LRU_CONV, dr)), _const_spec((1, dr)),
            _const_spec((LRU_BLOCKS, bw, 2 * bw)), _const_spec((1, dr)), _const_spec((1, dr)),
            _const_spec((1, dr)),
            _const_spec((dr, d)), _const_spec((1, d)),
        ],
        out_specs=[
            pl.BlockSpec((sb, tt, d), lambda b, t: (b, t, 0)),
            pl.BlockSpec((sb, LRU_CONV - 1, dr), lambda b, t: (b, 0, 0)),
            pl.BlockSpec((sb, 1, dr), lambda b, t: (b, 0, 0)),
        ],
        out_shape=[
            jax.ShapeDtypeStruct((bn, T, d), F32),
            jax.ShapeDtypeStruct((bn, LRU_CONV - 1, dr), F32),
            jax.ShapeDtypeStruct((bn, 1, dr), F32),
        ],
        scratch_shapes=[
            pltpu.VMEM((sb, tt + 8, dr), F32),
            pltpu.VMEM((sb * tt, dr), F32),
            pltpu.VMEM((sb * tt, dr), F32),
            pltpu.VMEM((sb * tt, dr), F32),
            pltpu.VMEM((sb * tt, dr), F32),
            pltpu.VMEM((sb, 8, dr), F32),
        ],
        compiler_params=_params(("arbitrary", "arbitrary")),
        name="lru",
    )(x, mods, ng, cst, hst.reshape(bn, 1, dr), p["wx"], row(p["bx"]), p["wy"], row(p["by"]),
      p["cw"], row(p["cb"]), p["wai"], row(p["ba"]), row(p["bi"]), row(p["lam"]),
      p["wo"], row(p["bo"]))
    return out, cst_new, hst_new.reshape(bn, dr)


def _ssd_kernel(x_ref, mod_ref, ng_ref, cst_ref, sst_ref, wz_ref, wxbc_ref, wdt_ref, cw_ref, cb_ref,
                dtb_ref, alog_ref, dsk_ref, sng_ref, wout_ref, e2_ref,
                o_ref, cst_out_ref, sst_out_ref, buf, xbc_s, cum_s, ce_s, dte_s, y_s, ST, *, sb, tt):
    t = pl.program_id(1)
    nt = pl.num_programs(1)
    d = x_ref.shape[-1]
    di = wz_ref.shape[-1]
    cdim = wxbc_ref.shape[-1]
    w = SSD_CONV
    L = SSD_CHUNK
    P = SSD_HEAD_DIM
    N = SSD_STATE
    G = SSD_GROUPS
    hpg = di // P // G
    rows = sb * tt

    @pl.when(t == 0)
    def _():
        buf[:, 0:8, :] = jnp.zeros((sb, 8, cdim), F32)
        buf[:, 8 - (w - 1):8, :] = cst_ref[...]
        for s in range(sb):
            ST[s] = sst_ref[s].T

    x = x_ref[...]
    m = mod_ref[...]
    h = _rms(x, ng_ref[0:1, :]) * (1.0 + m[:, 1:2, :]) + m[:, 0:1, :]
    hb = h.reshape(rows, d).astype(BF16)
    pre = _dot(hb, wxbc_ref[...]).reshape(sb, tt, cdim)
    buf[:, 8:8 + tt, :] = pre
    cv = _conv_from_buf(buf, pre, cw_ref, cb_ref, w, tt).reshape(rows, cdim)
    cst_out_ref[...] = buf[:, 8 + tt - (w - 1):8 + tt, :]
    buf[:, 0:8, :] = buf[:, tt:tt + 8, :]
    xbc_s[...] = cv * jax.nn.sigmoid(cv)

    dt = jax.nn.softplus(_dot(hb, wdt_ref[...]) + dtb_ref[...])
    a_neg = -jnp.exp(alog_ref[...])
    ri = lax.broadcasted_iota(jnp.int32, (rows, rows), 0)
    ci = lax.broadcasted_iota(jnp.int32, (rows, rows), 1)
    tri = jnp.where((ri >= ci) & (ri // L == ci // L), 1.0, 0.0).astype(BF16)
    cum = _dot_exact_lhs(tri, dt * a_neg)
    cum_s[...] = cum

    def expand(v):
        hi = v.astype(BF16)
        mid = (v - hi.astype(F32)).astype(BF16)
        return _dot(jnp.concatenate([hi, mid], axis=1), e2_ref[...])

    ce_s[...] = expand(cum)
    dte_s[...] = expand(dt)

    nck = tt // L
    lane2 = lax.broadcasted_iota(jnp.int32, (L, LANES), 1)
    low_b = lane2 < P
    tril2 = lax.broadcasted_iota(jnp.int32, (L, LANES), 0) >= (lane2 & (P - 1))
    gw = hpg * P

    def chunk(idx, carry):
        s = idx // nck
        r0 = pl.multiple_of(idx * L, L)
        rws = pl.ds(r0, L)
        cum_t = cum_s[rws, :].T
        for g in range(G):
            Bm = xbc_s[rws, di + g * N:di + (g + 1) * N]
            Cm = xbc_s[rws, di + G * N + g * N:di + G * N + (g + 1) * N]
            Bb = Bm.astype(BF16)
            Cb = Cm.astype(BF16)
            cb2 = _dot_nt(Cb, jnp.concatenate([Bb, Bb], axis=0))
            st_g = ST[s, :, g * gw:(g + 1) * gw]
            yo_g = _dot(Cb, st_g.astype(BF16))
            xws = []
            for pr in range(hpg // 2):
                j = g * (hpg // 2) + pr
                sl = slice(j * LANES, (j + 1) * LANES)
                ce2 = ce_s[rws, sl]
                xdt2 = xbc_s[rws, sl] * dte_s[rws, sl]
                crow2 = jnp.concatenate([cum_t[2 * j:2 * j + 1, :], cum_t[2 * j + 1:2 * j + 2, :]], axis=1)
                m2 = (cb2 * jnp.exp(jnp.where(tril2, ce2 - crow2, -jnp.inf))).astype(BF16)
                xb = xdt2.astype(BF16)
                zb = jnp.zeros_like(xb)
                rhs = jnp.concatenate([jnp.where(low_b, xb, zb), jnp.where(low_b, zb, xb)], axis=0)
                yd2 = _dot(m2, rhs)
                y_s[rws, sl] = yd2 + yo_g[:, pr * LANES:(pr + 1) * LANES] * jnp.exp(ce2)
                xws.append((xdt2 * jnp.exp(ce2[L - 1:L, :] - ce2)).astype(BF16))
            xw_g = jnp.concatenate(xws, axis=1)
            dec_g = jnp.exp(ce_s[pl.ds(r0 + L - 1, 1), g * gw:(g + 1) * gw])
            ST[s, :, g * gw:(g + 1) * gw] = st_g * dec_g + _dot(Bm.T.astype(BF16), xw_g)
        return carry

    lax.fori_loop(0, sb * nck, chunk, 0)

    @pl.when(t == nt - 1)
    def _():
        for s in range(sb):
            sst_out_ref[s] = ST[s].T

    z = _dot(hb, wz_ref[...])
    y = y_s[...] + dsk_ref[...] * xbc_s[:, 0:di]
    y = _rms(y * (z * jax.nn.sigmoid(z)), sng_ref[...])
    yo = _dot(y.astype(BF16), wout_ref[...]).reshape(sb, tt, d)
    o_ref[...] = x + m[:, 2:3, :] * _rms(yo, ng_ref[1:2, :])


def _ssd_call(x, mods, ng, cst, sst, p, sb, tt):
    bn, T, d = x.shape
    di = p["wz"].shape[-1]
    cdim = p["wxbc"].shape[-1]
    nh = di // SSD_HEAD_DIM
    kern = functools.partial(_ssd_kernel, sb=sb, tt=tt)
    out, cst_new, sst_new = pl.pallas_call(
        kern,
        grid=(bn // sb, T // tt),
        in_specs=[
            pl.BlockSpec((sb, tt, d), lambda b, t: (b, t, 0)),
            pl.BlockSpec((sb, 6, d), lambda b, t: (b, 0, 0)),
            _const_spec((4, d)),
            pl.BlockSpec((sb, SSD_CONV - 1, cdim), lambda b, t: (b, 0, 0)),
            pl.BlockSpec((sb, di, SSD_STATE), lambda b, t: (b, 0, 0)),
            _const_spec((d, di)), _const_spec((d, cdim)), _const_spec((d, LANES)),
            _const_spec((SSD_CONV, cdim)), _const_spec((1, cdim)),
            _const_spec((1, LANES)), _const_spec((1, LANES)),
            _const_spec((1, di)), _const_spec((1, di)),
            _const_spec((di, d)), _const_spec((2 * LANES, di)),
        ],
        out_specs=[
            pl.BlockSpec((sb, tt, d), lambda b, t: (b, t, 0)),
            pl.BlockSpec((sb, SSD_CONV - 1, cdim), lambda b, t: (b, 0, 0)),
            pl.BlockSpec((sb, di, SSD_STATE), lambda b, t: (b, 0, 0)),
        ],
        out_shape=[
            jax.ShapeDtypeStruct((bn, T, d), F32),
            jax.ShapeDtypeStruct((bn, SSD_CONV - 1, cdim), F32),
            jax.ShapeDtypeStruct((bn, di, SSD_STATE), F32),
        ],
        scratch_shapes=[
            pltpu.VMEM((sb, tt + 8, cdim), F32),
            pltpu.VMEM((sb * tt, cdim), F32),
            pltpu.VMEM((sb * tt, LANES), F32),
            pltpu.VMEM((sb * tt, di), F32),
            pltpu.VMEM((sb * tt, di), F32),
            pltpu.VMEM((sb * tt, di), F32),
            pltpu.VMEM((sb, SSD_STATE, di), F32),
        ],
        compiler_params=_params(("arbitrary", "arbitrary")),
        name="ssd",
    )(x, mods, ng, cst, sst.reshape(bn, di, SSD_STATE), p["wz"], p["wxbc"], p["wdt"], p["cw"], p["cb"],
      p["dtb"], p["alog"], p["dsk"], p["sng"], p["wout"], p["e2"])
    return out, cst_new, sst_new.reshape(bn, nh, SSD_HEAD_DIM, SSD_STATE)


def _forget_lanes(f):
    nh = FOX_HEADS
    lane = lax.broadcasted_iota(jnp.int32, f.shape, 1)
    hi, mid, lo = (p.astype(F32) for p in _split3(jnp.where(lane < nh, f, 0.0)))
    parts = hi + pltpu.roll(mid, nh, 1) + pltpu.roll(lo, 2 * nh, 1)
    one = jnp.ones_like(f)
    zero = jnp.zeros_like(f)
    qa = jnp.where(lane < 3 * nh, one, jnp.where(lane < 6 * nh, pltpu.roll(parts, 3 * nh, 1), zero))
    ka = jnp.where(lane < 3 * nh, -parts, jnp.where(lane < 6 * nh, one, zero))
    return qa.astype(BF16), ka.astype(BF16)


def _fox_proj_kernel(*refs, sb, tt, past):
    if past:
        (x_ref, mod_ref, ng_ref, lfp_ref, wqkv_ref, wf_ref, bf_ref,
         q_ref, k_ref, v_ref, kb_ref, vb_ref, lf_ref, fq_ref, fp_ref, fcar) = refs
    else:
        (x_ref, mod_ref, ng_ref, wqkv_ref, wf_ref, bf_ref,
         q_ref, k_ref, v_ref, kb_ref, vb_ref, lf_ref, qa_ref, ka_ref, fcar) = refs
    t = pl.program_id(1)
    d = x_ref.shape[-1]
    nh = lf_ref.shape[-1]
    rows = sb * tt
    cblk = min(tt, 256)
    trt = jnp.where(_tril(cblk), 1.0, 0.0).astype(BF16)

    def cumsum_rows(get_rows, n, car):
        outs = []
        for r in range(0, n, cblk):
            c = _dot_exact_lhs(trt, get_rows(r, cblk)) + car
            outs.append(c)
            car = c[cblk - 1:cblk, :]
        return outs, car

    @pl.when(t == 0)
    def _():
        if past:
            plen = lfp_ref.shape[1]
            for s in range(sb):
                outs, car = cumsum_rows(lambda r, n, s=s: lfp_ref[s, r:r + n, :], plen, jnp.zeros((1, LANES), F32))
                for bi, c in enumerate(outs):
                    fp_ref[s, bi * cblk:(bi + 1) * cblk, :] = c[:, 0:nh] * LOG2E
                fcar[s] = jnp.broadcast_to(car, (8, LANES))
        else:
            fcar[...] = jnp.zeros((sb, 8, LANES), F32)

    x = x_ref[...]
    m = mod_ref[...]
    h = _rms(x, ng_ref[0:1, :]) * (1.0 + m[:, 1:2, :]) + m[:, 0:1, :]
    hb = h.reshape(rows, d).astype(BF16)
    qkv = _dot(hb, wqkv_ref[...])
    q = qkv[:, 0:d] * (FOX_HEAD_DIM ** -0.5 * LOG2E)
    k = qkv[:, d:2 * d]
    v = qkv[:, 2 * d:3 * d]
    q_ref[...] = q.astype(BF16).reshape(sb, tt, d)
    k_ref[...] = k.reshape(sb, tt, d)
    v_ref[...] = v.reshape(sb, tt, d)
    kb_ref[...] = k.astype(BF16).reshape(sb, tt, d)
    if past:
        vb_ref[...] = v.astype(BF16).reshape(sb, tt, d)
    else:
        for s in range(sb):
            vb_ref[s] = v[s * tt:(s + 1) * tt, :].T.astype(BF16)
    lf = jax.nn.log_sigmoid(_dot(hb, wf_ref[...]) + bf_ref[...])
    lf_ref[...] = lf[:, 0:nh].reshape(sb, tt, nh)
    for s in range(sb):
        outs, car = cumsum_rows(lambda r, n, s=s: lf[s * tt + r:s * tt + r + n, :], tt, fcar[s, 0:1, :])
        for bi, c in enumerate(outs):
            if past:
                fq_ref[s, bi * cblk:(bi + 1) * cblk, :] = c[:, 0:nh] * LOG2E
            else:
                qa, ka = _forget_lanes(c * LOG2E)
                qa_ref[s, bi * cblk:(bi + 1) * cblk, :] = qa
                ka_ref[s, bi * cblk:(bi + 1) * cblk, :] = ka
        fcar[s] = jnp.broadcast_to(car, (8, LANES))


def _fox_proj_call(x, mods, ng, lf_past, wqkv, wf, bf, sb, tt):
    bn, T, d = x.shape
    nh = FOX_HEADS
    past = lf_past is not None
    kern = functools.partial(_fox_proj_kernel, sb=sb, tt=tt, past=past)
    in_specs = [
        pl.BlockSpec((sb, tt, d), lambda b, t: (b, t, 0)),
        pl.BlockSpec((sb, 6, d), lambda b, t: (b, 0, 0)),
        _const_spec((4, d)),
    ]
    args = [x, mods, ng]
    if past:
        plen = lf_past.shape[1]
        in_specs.append(pl.BlockSpec((sb, plen, LANES), lambda b, t: (b, 0, 0)))
        args.append(jnp.pad(lf_past, ((0, 0), (0, 0), (0, LANES - nh))))
    in_specs += [_const_spec((d, 3 * d)), _const_spec((d, LANES)), _const_spec((1, LANES))]
    args += [wqkv, wf, jnp.pad(bf.reshape(1, nh), ((0, 0), (0, LANES - nh)))]
    tile = pl.BlockSpec((sb, tt, d), lambda b, t: (b, t, 0))
    small = pl.BlockSpec((sb, tt, nh), lambda b, t: (b, t, 0))
    out_specs = [tile, tile, tile, tile, tile, small, small]
    out_shape = [
        jax.ShapeDtypeStruct((bn, T, d), BF16),
        jax.ShapeDtypeStruct((bn, T, d), F32),
        jax.ShapeDtypeStruct((bn, T, d), F32),
        jax.ShapeDtypeStruct((bn, T, d), BF16),
        jax.ShapeDtypeStruct((bn, T, d), BF16),
        jax.ShapeDtypeStruct((bn, T, nh), F32),
        jax.ShapeDtypeStruct((bn, T, nh), F32),
    ]
    if past:
        out_specs.append(pl.BlockSpec((sb, plen, nh), lambda b, t: (b, 0, 0)))
        out_shape.append(jax.ShapeDtypeStruct((bn, plen, nh), F32))
    else:
        out_specs[4] = pl.BlockSpec((sb, d, tt), lambda b, t: (b, 0, t))
        out_shape[4] = jax.ShapeDtypeStruct((bn, d, T), BF16)
        wide = pl.BlockSpec((sb, tt, LANES), lambda b, t: (b, t, 0))
        out_specs[6:] = [wide, wide]
        out_shape[6:] = [jax.ShapeDtypeStruct((bn, T, LANES), BF16)] * 2
    return pl.pallas_call(
        kern,
        grid=(bn // sb, T // tt),
        in_specs=in_specs,
        out_specs=out_specs,
        out_shape=out_shape,
        scratch_shapes=[pltpu.VMEM((sb, 8, LANES), F32)],
        compiler_params=_params(("arbitrary", "arbitrary")),
        name="fox_proj",
    )(*args)


def _fox_attn_kernel(x_ref, mod_ref, ng_ref, q_ref, k_ref, v_ref, fq_ref, fkt_ref, wo_ref,
                     o_ref, acc, lacc, m_s, fq_s, *, tq, tk, q_off, k_pad, rs):
    qi = pl.program_id(1)
    ki = pl.program_id(2)
    nk = pl.num_programs(2)
    d = x_ref.shape[-1]
    npair = d // LANES
    hd = FOX_HEAD_DIM
    q_lo = q_off + qi * tq
    k_lo = ki * tk
    nstrip = tq // rs

    @pl.when(ki == 0)
    def _():
        acc[...] = jnp.zeros(acc.shape, F32)
        lacc[...] = jnp.zeros(lacc.shape, F32)
        m_s[...] = jnp.full(m_s.shape, -1e30, F32)
        for hh in range(2 * npair):
            fq_s[hh] = jnp.broadcast_to(fq_ref[:, hh:hh + 1], (tq, LANES))

    low = lax.broadcasted_iota(jnp.int32, (tq, LANES), 1) < hd
    low_k = lax.broadcasted_iota(jnp.int32, (tk, LANES), 1) < hd

    def step(masked):
        ncb = tk // LANES
        if masked:
            col = lax.broadcasted_iota(jnp.int32, (rs, LANES), 1)
            dcol = col - lax.broadcasted_iota(jnp.int32, (rs, LANES), 0)

        def block_kind(i, c):
            if not masked:
                return 2
            first_vis, last_vis = q_off + i * rs, q_off + i * rs + rs - 1
            if c * LANES > last_vis or (c + 1) * LANES <= k_pad:
                return 0
            if c * LANES + LANES - 1 <= first_vis and c * LANES >= k_pad:
                return 2
            return 1

        def logits(s, fk, i, c):
            t = s[i * rs:(i + 1) * rs, c * LANES:(c + 1) * LANES] - fk[:, c * LANES:(c + 1) * LANES]
            if block_kind(i, c) == 1:
                ok = dcol <= (q_off + i * rs - c * LANES)
                if c * LANES < k_pad:
                    ok = ok & (col >= k_pad - c * LANES)
                t = jnp.where(ok, t, -jnp.inf)
            return t

        for j in range(npair):
            q2 = q_ref[:, j * LANES:(j + 1) * LANES]
            k2 = k_ref[:, j * LANES:(j + 1) * LANES]
            v2 = v_ref[:, j * LANES:(j + 1) * LANES]
            ones = jnp.ones_like(v2)
            vaug = (jnp.where(low_k, v2, ones), jnp.where(low_k, ones, v2))
            outs, alphas = [], []
            for half in range(2):
                hh = 2 * j + half
                qh = jnp.where(low if half == 0 else ~low, q2, jnp.zeros_like(q2))
                s = _dot_nt(qh, k2)
                fk = fkt_ref[hh:hh + 1, :]
                fq = fq_s[hh]
                m_old = m_s[hh]

                pms = []
                for i in range(nstrip):
                    pm = None
                    for c in range(ncb):
                        if block_kind(i, c) > 0:
                            t = logits(s, fk, i, c)
                            pm = t if pm is None else jnp.maximum(pm, t)
                    pms.append(pm)
                mx = jnp.max(jnp.concatenate(pms, axis=0), axis=-1, keepdims=True)
                m_new = jnp.maximum(m_old, mx + fq)
                alpha = jnp.exp2(m_old - m_new)
                c_b = m_new - fq
                prow = []
                for i in range(nstrip):
                    cbs = c_b[i * rs:(i + 1) * rs]
                    pieces = []
                    for c in range(ncb):
                        if block_kind(i, c) > 0:
                            pieces.append(jnp.exp2(logits(s, fk, i, c) - cbs).astype(BF16))
                        else:
                            pieces.append(jnp.zeros((rs, LANES), BF16))
                    prow.append(jnp.concatenate(pieces, axis=1))
                m_s[hh] = m_new
                outs.append(_dot(jnp.concatenate(prow, axis=0), vaug[half]))
                alphas.append(alpha)
            sl = slice(j * LANES, (j + 1) * LANES)
            acc[:, sl] = jnp.where(low, alphas[0], alphas[1]) * acc[:, sl] + jnp.where(low, outs[0], outs[1])
            lacc[:, sl] = jnp.where(low, alphas[1], alphas[0]) * lacc[:, sl] + jnp.where(low, outs[1], outs[0])

    needed = k_lo <= q_lo + tq - 1
    need_mask = (k_lo + tk - 1 > q_lo) | (k_lo < k_pad)

    @pl.when(needed & need_mask)
    def _():
        step(True)

    @pl.when(needed & jnp.logical_not(need_mask))
    def _():
        step(False)

    @pl.when(ki == nk - 1)
    def _():
        cols = []
        for j in range(npair):
            l2 = pltpu.roll(lacc[:, j * LANES:(j + 1) * LANES], hd, 1)
            cols.append((acc[:, j * LANES:(j + 1) * LANES] / l2).astype(BF16))
        o = jnp.concatenate(cols, axis=1)
        y = _dot(o, wo_ref[...])
        m = mod_ref[...]
        o_ref[...] = x_ref[...] + m[2:3, :] * _rms(y, ng_ref[1:2, :])


def _fox_attn_call(x, mods, ng, q, kb, vb, fq, fkt, wo, tq, tk, q_off, k_pad):
    bn, T, d = x.shape
    tkeys = kb.shape[1]
    nh = FOX_HEADS
    nq, nk = T // tq, tkeys // tk
    assert (nq == 1 and nk == 1) or (q_off == 0 and k_pad == 0 and tq == tk)
    rs = min(32, tq)
    kern = functools.partial(_fox_attn_kernel, tq=tq, tk=tk, q_off=q_off, k_pad=k_pad, rs=rs)

    def kmap(b, i, j):
        last = (q_off + i * tq + tq - 1) // tk
        return (b, jnp.minimum(j, last), 0)

    def fkmap(b, i, j):
        last = (q_off + i * tq + tq - 1) // tk
        return (b, 0, jnp.minimum(j, last))

    return pl.pallas_call(
        kern,
        grid=(bn, T // tq, tkeys // tk),
        in_specs=[
            pl.BlockSpec((None, tq, d), lambda b, i, j: (b, i, 0)),
            pl.BlockSpec((None, 6, d), lambda b, i, j: (b, 0, 0)),
            _const_spec((4, d)),
            pl.BlockSpec((None, tq, d), lambda b, i, j: (b, i, 0)),
            pl.BlockSpec((None, tk, d), kmap),
            pl.BlockSpec((None, tk, d), kmap),
            pl.BlockSpec((None, tq, nh), lambda b, i, j: (b, i, 0)),
            pl.BlockSpec((None, nh, tk), fkmap),
            _const_spec((d, d)),
        ],
        out_specs=pl.BlockSpec((None, tq, d), lambda b, i, j: (b, i, 0)),
        out_shape=jax.ShapeDtypeStruct((bn, T, d), F32),
        scratch_shapes=[
            pltpu.VMEM((tq, d), F32),
            pltpu.VMEM((tq, d), F32),
            pltpu.VMEM((nh, tq, LANES), F32),
            pltpu.VMEM((nh, tq, LANES), F32),
        ],
        compiler_params=_params(("arbitrary", "arbitrary", "arbitrary")),
        name="fox_attn",
    )(x, mods, ng, q, kb, vb, fq, fkt, wo)


def _fox_attn_t_kernel(x_ref, mod_ref, ng_ref, q_ref, k_ref, vt_ref, qa_ref, ka_ref, wo_ref,
                       o_ref, acct, m_s, *, tq, tk, rs):
    qi = pl.program_id(1)
    ki = pl.program_id(2)
    nk = pl.num_programs(2)
    d = x_ref.shape[-1]
    npair = d // LANES
    hd = FOX_HEAD_DIM
    nstrip = tk // rs
    nqb = tq // LANES

    @pl.when(ki == 0)
    def _():
        acct[...] = jnp.zeros(acct.shape, F32)
        m_s[...] = jnp.full(m_s.shape, -1e30, F32)

    lane_q = lax.broadcasted_iota(jnp.int32, (tq, LANES), 1)
    low = lane_q < hd
    ones_v = jnp.ones((hd, tk), BF16)

    def step(masked):
        if masked:
            dk = (lax.broadcasted_iota(jnp.int32, (rs, LANES), 0)
                  - lax.broadcasted_iota(jnp.int32, (rs, LANES), 1))
        qa = qa_ref[...]
        ka = ka_ref[...]

        def block_kind(i, c):
            if not masked:
                return 2
            if i * rs > c * LANES + LANES - 1:
                return 0
            if i * rs + rs - 1 <= c * LANES:
                return 2
            return 1

        def logits(st, i, c):
            t = st[i * rs:(i + 1) * rs, c * LANES:(c + 1) * LANES]
            if block_kind(i, c) == 1:
                t = jnp.where(dk <= (c * LANES - i * rs), t, -jnp.inf)
            return t

        def qk(hh):
            j, half = hh // 2, hh % 2
            q2 = q_ref[:, j * LANES:(j + 1) * LANES]
            qh = jnp.where(low if half == 0 else ~low, q2, jnp.zeros_like(q2))
            sel = ((lane_q & (FOX_HEADS - 1)) == hh) & (lane_q < 6 * FOX_HEADS)
            qcat = jnp.concatenate([qh, jnp.where(sel, qa, jnp.zeros_like(qa))], axis=1)
            kcat = jnp.concatenate([k_ref[:, j * LANES:(j + 1) * LANES], ka], axis=1)
            return _dot_nt(kcat, qcat)

        ahead = 2
        sts = [qk(h0) for h0 in range(ahead)]
        for hh in range(2 * npair):
            j, half = hh // 2, hh % 2
            st = sts.pop(0)
            if hh + ahead < 2 * npair:
                sts.append(qk(hh + ahead))
            vt2 = vt_ref[j * LANES:(j + 1) * LANES, :]
            m_old = m_s[hh:hh + 1, :]
            pm = [None] * nqb
            for i in range(nstrip):
                for c in range(nqb):
                    if block_kind(i, c) > 0:
                        t = logits(st, i, c)
                        pm[c] = t if pm[c] is None else jnp.maximum(pm[c], t)
            mx = jnp.concatenate([jnp.max(p_, axis=0, keepdims=True) for p_ in pm], axis=1)
            m_new = jnp.maximum(m_old, mx)
            alpha = jnp.exp2(m_old - m_new)
            prow = []
            for i in range(nstrip):
                pieces = []
                for c in range(nqb):
                    if block_kind(i, c) > 0:
                        pieces.append(jnp.exp2(logits(st, i, c)
                                               - m_new[:, c * LANES:(c + 1) * LANES]).astype(BF16))
                    else:
                        pieces.append(jnp.zeros((rs, LANES), BF16))
                prow.append(jnp.concatenate(pieces, axis=1))
            vaug = jnp.concatenate([vt2[half * hd:(half + 1) * hd, :], ones_v], axis=0)
            ot = _dot(vaug, jnp.concatenate(prow, axis=0))
            acct[hh] = alpha * acct[hh] + ot
            m_s[hh:hh + 1, :] = m_new

    @pl.when(ki == qi)
    def _():
        step(True)

    @pl.when(ki < qi)
    def _():
        step(False)

    @pl.when(ki == nk - 1)
    def _():
        ots = []
        for hh in range(2 * npair):
            a = acct[hh]
            ots.append(a[0:hd] / a[hd:hd + 1])
        o = jnp.concatenate(ots, axis=0).T.astype(BF16)
        y = _dot(o, wo_ref[...])
        m = mod_ref[...]
        o_ref[...] = x_ref[...] + m[2:3, :] * _rms(y, ng_ref[1:2, :])


def _fox_attn_t_call(x, mods, ng, q, kb, vt, qa, ka, wo, tq):
    bn, T, d = x.shape
    nh = FOX_HEADS
    tk = tq
    kern = functools.partial(_fox_attn_t_kernel, tq=tq, tk=tk, rs=32)
    return pl.pallas_call(
        kern,
        grid=(bn, T // tq, T // tk),
        in_specs=[
            pl.BlockSpec((None, tq, d), lambda b, i, j: (b, i, 0)),
            pl.BlockSpec((None, 6, d), lambda b, i, j: (b, 0, 0)),
            _const_spec((4, d)),
            pl.BlockSpec((None, tq, d), lambda b, i, j: (b, i, 0)),
            pl.BlockSpec((None, tk, d), lambda b, i, j: (b, jnp.minimum(j, i), 0)),
            pl.BlockSpec((None, d, tk), lambda b, i, j: (b, 0, jnp.minimum(j, i))),
            pl.BlockSpec((None, tq, LANES), lambda b, i, j: (b, i, 0)),
            pl.BlockSpec((None, tk, LANES), lambda b, i, j: (b, jnp.minimum(j, i), 0)),
            _const_spec((d, d)),
        ],
        out_specs=pl.BlockSpec((None, tq, d), lambda b, i, j: (b, i, 0)),
        out_shape=jax.ShapeDtypeStruct((bn, T, d), F32),
        scratch_shapes=[
            pltpu.VMEM((nh, LANES, tq), F32),
            pltpu.VMEM((nh, tq), F32),
        ],
        compiler_params=_params(("arbitrary", "arbitrary", "arbitrary")),
        name="fox_attn_t",
    )(x, mods, ng, q, kb, vt, qa, ka, wo)


def _prep_weights(W):
    bf = lambda a: a.astype(BF16)
    di = W["ssd_w_out"].shape[0]
    nh = di // SSD_HEAD_DIM
    cdim = W["ssd_conv_w"].shape[-1]
    d = W["ssd_w_in"].shape[0]
    lru = []
    for j in range(W["lru_w_x"].shape[0]):
        lru.append(dict(
            wx=bf(W["lru_w_x"][j]), bx=W["lru_b_x"][j], wy=bf(W["lru_w_y"][j]), by=W["lru_b_y"][j],
            cw=W["lru_conv_w"][j], cb=W["lru_conv_b"][j],
            wai=bf(0.5 * jnp.concatenate([W["lru_w_a"][j], W["lru_w_i"][j]], axis=-1)),
            ba=0.5 * W["lru_b_a"][j], bi=0.5 * W["lru_b_i"][j], lam=W["lru_lambda"][j],
            wo=bf(W["lru_w_o"][j]), bo=W["lru_b_o"][j]))
    w_in = W["ssd_w_in"]
    pad_l = lambda v: jnp.pad(v.reshape(1, -1), ((0, 0), (0, LANES - v.shape[-1])))
    onehot = (jnp.arange(LANES)[:, None] == jnp.arange(di)[None, :] // SSD_HEAD_DIM).astype(BF16)
    ssd = dict(
        e2=jnp.concatenate([onehot, onehot], axis=0),
        wz=bf(w_in[:, :di]), wxbc=bf(w_in[:, di:di + cdim]),
        wdt=bf(jnp.pad(w_in[:, di + cdim:], ((0, 0), (0, LANES - nh)))),
        cw=W["ssd_conv_w"], cb=W["ssd_conv_b"].reshape(1, cdim),
        dtb=pad_l(W["ssd_dt_bias"]), alog=pad_l(W["ssd_a_log"]),
        dsk=jnp.repeat(W["ssd_d"], SSD_HEAD_DIM).reshape(1, di), sng=W["ssd_norm_g"].reshape(1, di),
        wout=bf(W["ssd_w_out"]))
    fox = dict(
        wqkv=bf(W["fox_w_qkv"]),
        wf=bf(jnp.pad(W["fox_w_f"], ((0, 0), (0, LANES - W["fox_w_f"].shape[-1])))),
        bf=W["fox_b_f"], wo=bf(W["fox_w_o"]))
    ffn = [dict(wg=bf(W["ffn_w_gate"][l]), wu=bf(W["ffn_w_up"][l]), cw=W["ffn_conv_w"][l],
                cb=W["ffn_conv_b"][l], wd=bf(W["ffn_w_down"][l])) for l in range(W["ffn_w_gate"].shape[0])]
    return dict(lru=lru, ssd=ssd, fox=fox, ffn=ffn, norm_g=W["norm_g"])


def _run_trunk(x, mods, prev, P, cfg):
    bn, T, d = x.shape
    sb, tt, tt_ffn, tq, tk = cfg["sb"], cfg["tt"], cfg["tt_ffn"], cfg["tq"], cfg["tk"]
    depth = mods.shape[0]
    new = {}
    ffn_bufs = []
    for layer in range(depth):
        ng = P["norm_g"][layer]
        ml = mods[layer]
        kind = layer % 3
        tag = "l%d" % layer
        if kind == 0:
            x, cn, hn = _lru_call(x, ml, ng, prev[tag + "_conv"], prev[tag + "_h"], P["lru"][layer // 3], sb, tt)
            new[tag + "_conv"], new[tag + "_h"] = cn, hn
        elif kind == 1:
            x, cn, sn = _ssd_call(x, ml, ng, prev[tag + "_conv"], prev[tag + "_ssm"], P["ssd"], cfg["sb_ssd"], tt)
            new[tag + "_conv"], new[tag + "_ssm"] = cn, sn
        else:
            fx = P["fox"]
            kp, vp, lp = prev.get(tag + "_k"), prev.get(tag + "_v"), prev.get(tag + "_logf")
            res = _fox_proj_call(x, ml, ng, lp, fx["wqkv"], fx["wf"], fx["bf"], sb, tt)
            q, k, v, kb, vb, lf, fq = res[:7]
            if kp is None:
                x = _fox_attn_t_call(x, ml, ng, q, kb, vb, res[6], res[7], fx["wo"], tq)
            else:
                plen = kp.shape[1]
                tk = -(-(plen + T) // LANES) * LANES
                k_pad = tk - (plen + T)
                zk = jnp.zeros((bn, k_pad, d), BF16)
                kb = jnp.concatenate([zk, kp.reshape(bn, plen, d).astype(BF16), kb], axis=1)
                vb = jnp.concatenate([zk, vp.reshape(bn, plen, d).astype(BF16), vb], axis=1)
                fk = jnp.concatenate([jnp.zeros((bn, k_pad, FOX_HEADS), F32), res[7], fq], axis=1)
                x = _fox_attn_call(x, ml, ng, q, kb, vb, fq, jnp.swapaxes(fk, 1, 2), fx["wo"], tq, tk,
                                   k_pad + plen, k_pad)
            new[tag + "_k"] = k.reshape(bn, T, FOX_HEADS, FOX_HEAD_DIM)
            new[tag + "_v"] = v.reshape(bn, T, FOX_HEADS, FOX_HEAD_DIM)
            new[tag + "_logf"] = lf
        f = P["ffn"][layer]
        x, buf = _ffn_call(x, ml, ng, prev["ffn_conv"][layer], f["wg"], f["wu"], f["cw"], f["cb"], f["wd"],
                           sb, tt_ffn)
        ffn_bufs.append(buf)
    new["ffn_conv"] = jnp.stack(ffn_bufs)
    return x, new


def _fresh_state(bn, W):
    dr = W["lru_w_x"].shape[-1]
    cdim = W["ssd_conv_w"].shape[-1]
    di = W["ssd_w_out"].shape[0]
    f = W["ffn_w_gate"].shape[-1]
    depth = W["ffn_w_gate"].shape[0]
    st = dict(ffn_conv=jnp.zeros((depth, bn, FFN_CONV - 1, f), F32))
    for layer in range(depth):
        tag = "l%d" % layer
        if layer % 3 == 0:
            st[tag + "_conv"] = jnp.zeros((bn, LRU_CONV - 1, dr), F32)
            st[tag + "_h"] = jnp.zeros((bn, dr), F32)
        elif layer % 3 == 1:
            st[tag + "_conv"] = jnp.zeros((bn, SSD_CONV - 1, cdim), F32)
            st[tag + "_ssm"] = jnp.zeros((bn, di // SSD_HEAD_DIM, SSD_HEAD_DIM, SSD_STATE), F32)
    return st


def _tile_cfg(bn, T):
    if T >= 256:
        return dict(sb=1, sb_ssd=1, tt=256, tt_ffn=256, tq=512 if T % 512 == 0 else 256,
                    tk=512 if T % 512 == 0 else 256)
    sb = 4 if bn % 4 == 0 else 1
    return dict(sb=sb, sb_ssd=2 if bn % 2 == 0 else 1, tt=T, tt_ffn=T, tq=T, tk=128)


def kernel(x_prompt, x_sample, c_prompt, c_sample, state_l0_conv, state_l0_h, state_l1_conv, state_l1_ssm, cache_l2_k, cache_l2_v, cache_l2_logf, state_l3_conv, state_l3_h, state_ffn_conv, ada_w, ada_b, norm_g, lru_w_x, lru_b_x, lru_w_y, lru_b_y, lru_conv_w, lru_conv_b, lru_w_a, lru_b_a, lru_w_i, lru_b_i, lru_lambda, lru_w_o, lru_b_o, ssd_w_in, ssd_conv_w, ssd_conv_b, ssd_dt_bias, ssd_a_log, ssd_d, ssd_norm_g, ssd_w_out, fox_w_qkv, fox_w_f, fox_b_f, fox_w_o, ffn_w_gate, ffn_w_up, ffn_conv_w, ffn_conv_b, ffn_w_down):
    W = dict(ada_w=ada_w, ada_b=ada_b, norm_g=norm_g,
             lru_w_x=lru_w_x, lru_b_x=lru_b_x, lru_w_y=lru_w_y, lru_b_y=lru_b_y,
             lru_conv_w=lru_conv_w, lru_conv_b=lru_conv_b, lru_w_a=lru_w_a, lru_b_a=lru_b_a,
             lru_w_i=lru_w_i, lru_b_i=lru_b_i, lru_lambda=lru_lambda, lru_w_o=lru_w_o, lru_b_o=lru_b_o,
             ssd_w_in=ssd_w_in, ssd_conv_w=ssd_conv_w, ssd_conv_b=ssd_conv_b, ssd_dt_bias=ssd_dt_bias,
             ssd_a_log=ssd_a_log, ssd_d=ssd_d, ssd_norm_g=ssd_norm_g, ssd_w_out=ssd_w_out,
             fox_w_qkv=fox_w_qkv, fox_w_f=fox_w_f, fox_b_f=fox_b_f, fox_w_o=fox_w_o,
             ffn_w_gate=ffn_w_gate, ffn_w_up=ffn_w_up, ffn_conv_w=ffn_conv_w, ffn_conv_b=ffn_conv_b,
             ffn_w_down=ffn_w_down)
    P = _prep_weights(W)
    bp, bs = x_prompt.shape[0], x_sample.shape[0]
    d = x_prompt.shape[-1]
    depth = ada_w.shape[0]
    mods = _ada_call(jnp.concatenate([c_prompt, c_sample], axis=0), ada_w, ada_b)
    mods = mods.reshape(depth, bp + bs, 6, d)

    y_prompt, p = _run_trunk(x_prompt, mods[:, :bp], _fresh_state(bp, W), P, _tile_cfg(bp, x_prompt.shape[1]))
    prev = dict(l0_conv=state_l0_conv, l0_h=state_l0_h, l1_conv=state_l1_conv, l1_ssm=state_l1_ssm,
                l2_k=cache_l2_k, l2_v=cache_l2_v, l2_logf=cache_l2_logf,
                l3_conv=state_l3_conv, l3_h=state_l3_h, ffn_conv=state_ffn_conv)
    y_sample, s = _run_trunk(x_sample, mods[:, bp:], prev, P, _tile_cfg(bs, x_sample.shape[1]))
    return (y_prompt, y_sample,
            p['l0_conv'], p['l0_h'], p['l1_conv'], p['l1_ssm'], p['l2_k'], p['l2_v'], p['l2_logf'],
            p['l3_conv'], p['l3_h'], p['ffn_conv'],
            s['l0_conv'], s['l0_h'], s['l1_conv'], s['l1_ssm'], s['l2_k'], s['l2_v'], s['l2_logf'],
            s['l3_conv'], s['l3_h'], s['ffn_conv'])
```

```python
import functools

import jax
import jax.numpy as jnp
from jax import lax
from jax.experimental import pallas as pl
from jax.experimental.pallas import tpu as pltpu

F32 = jnp.float32
BF16 = jnp.bfloat16

EPS = 1e-6
LOG2E = 1.4426950408889634
LRU_C = 8.0
LRU_BLOCKS = 8
LRU_CONV = 4
SSD_HEAD_DIM = 64
SSD_GROUPS = 4
SSD_STATE = 128
SSD_CONV = 4
SSD_CHUNK = 64
FOX_HEADS = 16
FOX_HEAD_DIM = 64
FFN_CONV = 3

LANES = 128
SUBLANES = 8
VMEM_LIMIT = 56 * 1024 * 1024


def _const_spec(shape):
    nd = len(shape)
    return pl.BlockSpec(shape, lambda *_: (0,) * nd, pipeline_mode=pl.Buffered(1))


def _params(sem):
    return pltpu.CompilerParams(dimension_semantics=sem, vmem_limit_bytes=VMEM_LIMIT)


def _rms(x, g):
    return x * lax.rsqrt(jnp.mean(x * x, axis=-1, keepdims=True) + EPS) * g


def _dot(a, b):
    return jnp.dot(a, b, preferred_element_type=F32)


def _dot_nt(a, b):
    return lax.dot_general(a, b, (((1,), (1,)), ((), ())), preferred_element_type=F32)


def _dot_tn(a, b):
    return lax.dot_general(a, b, (((0,), (0,)), ((), ())), preferred_element_type=F32)


def _split3(x):
    hi = x.astype(BF16)
    r1 = x - hi.astype(F32)
    mid = r1.astype(BF16)
    lo = (r1 - mid.astype(F32)).astype(BF16)
    return hi, mid, lo


def _dot_exact_lhs(m, x):
    hi, mid, lo = _split3(x)
    return _dot(m, hi) + (_dot(m, mid) + _dot(m, lo))


def _expm1_nonpos(x):
    u = jnp.exp(x)
    small = (u - 1.0) * x / jnp.log(u)
    return jnp.where(x < -1.0, u - 1.0, jnp.where(u == 1.0, x, small))


def _gelu_tanh(x):
    c0 = 0.7978845608028654
    hx = 0.5 * x
    return hx + hx * jnp.tanh(x * (c0 + (c0 * 0.044715) * (x * x)))


def _tril(n):
    r = lax.broadcasted_iota(jnp.int32, (n, n), 0)
    c = lax.broadcasted_iota(jnp.int32, (n, n), 1)
    return r >= c


def _ada_kernel(c_ref, w_ref, b_ref, o_ref):
    c = c_ref[...]
    s = (c * jax.nn.sigmoid(c)).astype(BF16)
    o_ref[...] = _dot(s, w_ref[...].astype(BF16)) + b_ref[...]


def _ada_call(c_all, ada_w, ada_b):
    depth, d, n = ada_w.shape
    bn = c_all.shape[0]
    tn = 1536 if n % 1536 == 0 else n
    return pl.pallas_call(
        _ada_kernel,
        grid=(depth, n // tn),
        in_specs=[
            pl.BlockSpec((bn, d), lambda l, j: (0, 0)),
            pl.BlockSpec((None, d, tn), lambda l, j: (l, 0, j)),
            pl.BlockSpec((None, 1, tn), lambda l, j: (l, 0, j)),
        ],
        out_specs=pl.BlockSpec((None, bn, tn), lambda l, j: (l, 0, j)),
        out_shape=jax.ShapeDtypeStruct((depth, bn, n), F32),
        compiler_params=_params(("arbitrary", "arbitrary")),
        name="ada",
    )(c_all, ada_w, ada_b.reshape(depth, 1, n))


def _conv_from_buf(buf, cur, w_ref, b_ref, width, tt):
    y = b_ref[...] + w_ref[width - 1:width, :] * cur
    for j in range(1, width):
        y = y + w_ref[width - 1 - j:width - j, :] * buf[:, 8 - j:8 - j + tt, :]
    return y


def _ffn_kernel(x_ref, mod_ref, ng_ref, st_ref, wg_ref, wu_ref, cw_ref, cb_ref, wd_ref,
                o_ref, st_out_ref, buf, *, sb, tt):
    t = pl.program_id(1)
    d = x_ref.shape[-1]
    f = wg_ref.shape[-1]
    w = FFN_CONV

    @pl.when(t == 0)
    def _():
        buf[:, 0:8, :] = jnp.zeros((sb, 8, f), F32)
        buf[:, 8 - (w - 1):8, :] = st_ref[...]

    x = x_ref[...]
    m = mod_ref[...]
    h = _rms(x, ng_ref[2:3, :]) * (1.0 + m[:, 4:5, :]) + m[:, 3:4, :]
    hb = h.reshape(sb * tt, d).astype(BF16)
    gp = _dot(hb, wg_ref[...]).reshape(sb, tt, f)
    up = _dot(hb, wu_ref[...])
    buf[:, 8:8 + tt, :] = gp
    g = _conv_from_buf(buf, gp, cw_ref, cb_ref, w, tt)
    st_out_ref[...] = buf[:, 8 + tt - (w - 1):8 + tt, :]
    buf[:, 0:8, :] = buf[:, tt:tt + 8, :]
    act = (_gelu_tanh(g).reshape(sb * tt, f) * up).astype(BF16)
    y = _dot(act, wd_ref[...]).reshape(sb, tt, d)
    o_ref[...] = x + m[:, 5:6, :] * _rms(y, ng_ref[3:4, :])


def _ffn_call(x, mods, ng, st, wg, wu, cw, cb, wd, layer, sb, tt):
    bn, T, d = x.shape
    f = wg.shape[-1]
    kern = functools.partial(_ffn_kernel, sb=sb, tt=tt)

    def layer_spec(shape):
        return pl.BlockSpec((None,) + shape, lambda b, t: (layer, 0, 0), pipeline_mode=pl.Buffered(1))

    return pl.pallas_call(
        kern,
        grid=(bn // sb, T // tt),
        in_specs=[
            pl.BlockSpec((sb, tt, d), lambda b, t: (b, t, 0)),
            pl.BlockSpec((sb, 6, d), lambda b, t: (b, 0, 0)),
            _const_spec((4, d)),
            pl.BlockSpec((sb, FFN_CONV - 1, f), lambda b, t: (b, 0, 0)),
            layer_spec((d, f)),
            layer_spec((d, f)),
            _const_spec((FFN_CONV, f)),
            _const_spec((1, f)),
            layer_spec((f, d)),
        ],
        out_specs=[
            pl.BlockSpec((sb, tt, d), lambda b, t: (b, t, 0)),
            pl.BlockSpec((sb, FFN_CONV - 1, f), lambda b, t: (b, 0, 0)),
        ],
        out_shape=[
            jax.ShapeDtypeStruct((bn, T, d), F32),
            jax.ShapeDtypeStruct((bn, FFN_CONV - 1, f), F32),
        ],
        scratch_shapes=[pltpu.VMEM((sb, tt + 8, f), F32)],
        compiler_params=_params(("arbitrary", "arbitrary")),
        name="ffn",
    )(x, mods, ng, st, wg, wu, cw, cb.reshape(1, f), wd)


def _lru_kernel(x_ref, mod_ref, ng_ref, cst_ref, hst_ref, wx_ref, bx_ref, wy_ref, by_ref,
                cw_ref, cb_ref, wai_ref, ba_ref, bi_ref, lam_ref, wo_ref, bo_ref,
                o_ref, cst_out_ref, hst_out_ref, buf, za_s, zi_s, u_s, g_s, hcar, *, sb, tt):
    t = pl.program_id(1)
    d = x_ref.shape[-1]
    dr = wx_ref.shape[-1]
    bw = dr // LRU_BLOCKS
    w = LRU_CONV
    rows = sb * tt

    @pl.when(t == 0)
    def _():
        buf[:, 0:8, :] = jnp.zeros((sb, 8, dr), F32)
        buf[:, 8 - (w - 1):8, :] = cst_ref[...]
        hcar[...] = jnp.broadcast_to(hst_ref[...], (sb, 8, dr))

    x = x_ref[...]
    m = mod_ref[...]
    h = _rms(x, ng_ref[0:1, :]) * (1.0 + m[:, 1:2, :]) + m[:, 0:1, :]
    hb = h.reshape(rows, d).astype(BF16)
    xx = (_dot(hb, wx_ref[...]) + bx_ref[...]).reshape(sb, tt, dr)
    buf[:, 8:8 + tt, :] = xx
    u = _conv_from_buf(buf, xx, cw_ref, cb_ref, w, tt).reshape(rows, dr)
    cst_out_ref[...] = buf[:, 8 + tt - (w - 1):8 + tt, :]
    buf[:, 0:8, :] = buf[:, tt:tt + 8, :]

    u_s[...] = u
    ub = u.astype(BF16)
    for n in range(LRU_BLOCKS):
        z = _dot(ub[:, n * bw:(n + 1) * bw], wai_ref[n])
        za_s[:, n * bw:(n + 1) * bw] = z[:, :bw]
        zi_s[:, n * bw:(n + 1) * bw] = z[:, bw:]
    g_s[...] = _dot(hb, wy_ref[...]) + by_ref[...]

    lc = 256
    row8 = lax.broadcasted_iota(jnp.int32, (8, lc), 0)
    khalf = (-0.5 * LRU_C) * jax.nn.softplus(-lam_ref[...])
    for s in range(sb):
        def body(gi, car, s=s):
            r0 = pl.multiple_of(s * tt + gi * 8, 8)
            new_car = []
            for c in range(dr // lc):
                ls = slice(c * lc, (c + 1) * lc)
                kh = khalf[:, ls]
                log_a = kh + kh * jnp.tanh(za_s[pl.ds(r0, 8), ls] + ba_ref[:, ls])
                hu = 0.5 * u_s[pl.ds(r0, 8), ls]
                iu = hu + hu * jnp.tanh(zi_s[pl.ds(r0, 8), ls] + bi_ref[:, ls])
                A = jnp.exp(log_a)
                B = jnp.sqrt(jnp.tanh(-log_a) * (1.0 + A * A)) * iu
                for sh in (1, 2, 4):
                    As = pltpu.roll(A, sh, 0)
                    Bs = pltpu.roll(B, sh, 0)
                    msk = row8 >= sh
                    B = jnp.where(msk, A * Bs + B, B)
                    A = jnp.where(msk, A * As, A)
                H = A * car[c] + B
                g_s[pl.ds(r0, 8), ls] = H * _gelu_tanh(g_s[pl.ds(r0, 8), ls])
                new_car.append(jnp.broadcast_to(H[7:8, :], (8, lc)))
            return tuple(new_car)

        car0 = tuple(hcar[s, :, c * lc:(c + 1) * lc] for c in range(dr // lc))
        car = lax.fori_loop(0, tt // 8, body, car0, unroll=2)
        for c in range(dr // lc):
            hcar[s, :, c * lc:(c + 1) * lc] = car[c]
    hst_out_ref[...] = hcar[:, 7:8, :]

    y = (_dot(g_s[...].astype(BF16), wo_ref[...]) + bo_ref[...]).reshape(sb, tt, d)
    o_ref[...] = x + m[:, 2:3, :] * _rms(y, ng_ref[1:2, :])


def _lru_call(x, mods, ng, cst, hst, p, sb, tt):
    bn, T, d = x.shape
    dr = p["wx"].shape[-1]
    bw = dr // LRU_BLOCKS
    kern = functools.partial(_lru_kernel, sb=sb, tt=tt)
    row = lambda v: v.reshape(1, -1)
    out, cst_new, hst_new = pl.pallas_call(
        kern,
        grid=(bn // sb, T // tt),
        in_specs=[
            pl.BlockSpec((sb, tt, d), lambda b, t: (b, t, 0)),
            pl.BlockSpec((sb, 6, d), lambda b, t: (b, 0, 0)),
            _const_spec((4, d)),
            pl.BlockSpec((sb, LRU_CONV - 1, dr), lambda b, t: (b, 0, 0)),
            pl.BlockSpec((sb, 1, dr), lambda b, t: (b, 0, 0)),
            _const_spec((d, dr)), _const_spec((1, dr)),
            _const_spec((d, dr)), _const_spec((1, dr)),
            _const_spec((LRU_CONV, dr)), _const_spec((1, dr)),
            _const_spec((LRU_BLOCKS, bw, 2 * bw)), _const_spec((1, dr)), _const_spec((1, dr)),
            _const_spec((1, dr)),
            _const_spec((dr, d)), _const_spec((1, d)),
        ],
        out_specs=[
            pl.BlockSpec((sb, tt, d), lambda b, t: (b, t, 0)),
            pl.BlockSpec((sb, LRU_CONV - 1, dr), lambda b, t: (b, 0, 0)),
            pl.BlockSpec((sb, 1, dr), lambda b, t: (b, 0, 0)),
        ],
        out_shape=[
            jax.ShapeDtypeStruct((bn, T, d), F32),
            jax.ShapeDtypeStruct((bn, LRU_CONV - 1, dr), F32),
            jax.ShapeDtypeStruct((bn, 1, dr), F32),
        ],
        scratch_shapes=[
            pltpu.VMEM((sb, tt + 8, dr), F32),
            pltpu.VMEM((sb * tt, dr), F32),
            pltpu.VMEM((sb * tt, dr), F32),
            pltpu.VMEM((sb * tt, dr), F32),
            pltpu.VMEM((sb * tt, dr), F32),
            pltpu.VMEM((sb, 8, dr), F32),
        ],
        compiler_params=_params(("arbitrary", "arbitrary")),
        name="lru",
    )(x, mods, ng, cst, hst.reshape(bn, 1, dr), p["wx"], row(p["bx"]), p["wy"], row(p["by"]),
      p["cw"], row(p["cb"]), p["wai"], row(p["ba"]), row(p["bi"]), row(p["lam"]),
      p["wo"], row(p["bo"]))
    return out, cst_new, hst_new.reshape(bn, dr)


def _ssd_kernel(x_ref, mod_ref, ng_ref, cst_ref, sst_ref, wz_ref, wxbc_ref, wdt_ref, cw_ref, cb_ref,
                dtb_ref, alog_ref, dsk_ref, sng_ref, wout_ref, e2_ref,
                o_ref, cst_out_ref, sst_out_ref, buf, xbc_s, cum_s, ce_s, dte_s, y_s, ST, *, sb, tt):
    t = pl.program_id(1)
    nt = pl.num_programs(1)
    d = x_ref.shape[-1]
    di = wz_ref.shape[-1]
    cdim = wxbc_ref.shape[-1]
    w = SSD_CONV
    L = SSD_CHUNK
    P = SSD_HEAD_DIM
    N = SSD_STATE
    G = SSD_GROUPS
    hpg = di // P // G
    rows = sb * tt

    @pl.when(t == 0)
    def _():
        buf[:, 0:8, :] = jnp.zeros((sb, 8, cdim), F32)
        buf[:, 8 - (w - 1):8, :] = cst_ref[...]
        for s in range(sb):
            ST[s] = sst_ref[s].T

    x = x_ref[...]
    m = mod_ref[...]
    h = _rms(x, ng_ref[0:1, :]) * (1.0 + m[:, 1:2, :]) + m[:, 0:1, :]
    hb = h.reshape(rows, d).astype(BF16)
    pre = _dot(hb, wxbc_ref[...]).reshape(sb, tt, cdim)
    buf[:, 8:8 + tt, :] = pre
    cv = _conv_from_buf(buf, pre, cw_ref, cb_ref, w, tt).reshape(rows, cdim)
    cst_out_ref[...] = buf[:, 8 + tt - (w - 1):8 + tt, :]
    buf[:, 0:8, :] = buf[:, tt:tt + 8, :]
    xbc_s[...] = cv * jax.nn.sigmoid(cv)

    dt = jax.nn.softplus(_dot(hb, wdt_ref[...]) + dtb_ref[...])
    a_neg = -jnp.exp(alog_ref[...])
    ri = lax.broadcasted_iota(jnp.int32, (rows, rows), 0)
    ci = lax.broadcasted_iota(jnp.int32, (rows, rows), 1)
    tri = jnp.where((ri >= ci) & (ri // L == ci // L), 1.0, 0.0).astype(BF16)
    cum = _dot_exact_lhs(tri, dt * a_neg)
    cum_s[...] = cum

    def expand(v):
        hi = v.astype(BF16)
        mid = (v - hi.astype(F32)).astype(BF16)
        return _dot(jnp.concatenate([hi, mid], axis=1), e2_ref[...])

    ce_s[...] = expand(cum)
    dte_s[...] = expand(dt)

    nck = tt // L
    lane2 = lax.broadcasted_iota(jnp.int32, (L, LANES), 1)
    low_b = lane2 < P
    tril2 = lax.broadcasted_iota(jnp.int32, (L, LANES), 0) >= (lane2 & (P - 1))
    gw = hpg * P

    def chunk(idx, carry):
        s = idx // nck
        r0 = pl.multiple_of(idx * L, L)
        rws = pl.ds(r0, L)
        cum_t = cum_s[rws, :].T
        for g in range(G):
            Bm = xbc_s[rws, di + g * N:di + (g + 1) * N]
            Cm = xbc_s[rws, di + G * N + g * N:di + G * N + (g + 1) * N]
            Bb = Bm.astype(BF16)
            Cb = Cm.astype(BF16)
            cb2 = _dot_nt(Cb, jnp.concatenate([Bb, Bb], axis=0))
            st_g = ST[s, :, g * gw:(g + 1) * gw]
            yo_g = _dot(Cb, st_g.astype(BF16))
            xws = []
            for pr in range(hpg // 2):
                j = g * (hpg // 2) + pr
                sl = slice(j * LANES, (j + 1) * LANES)
                ce2 = ce_s[rws, sl]
                xdt2 = xbc_s[rws, sl] * dte_s[rws, sl]
                crow2 = jnp.concatenate([cum_t[2 * j:2 * j + 1, :], cum_t[2 * j + 1:2 * j + 2, :]], axis=1)
                m2 = (cb2 * jnp.exp(jnp.where(tril2, ce2 - crow2, -jnp.inf))).astype(BF16)
                xb = xdt2.astype(BF16)
                zb = jnp.zeros_like(xb)
                rhs = jnp.concatenate([jnp.where(low_b, xb, zb), jnp.where(low_b, zb, xb)], axis=0)
                yd2 = _dot(m2, rhs)
                y_s[rws, sl] = yd2 + yo_g[:, pr * LANES:(pr + 1) * LANES] * jnp.exp(ce2)
                xws.append((xdt2 * jnp.exp(ce2[L - 1:L, :] - ce2)).astype(BF16))
            xw_g = jnp.concatenate(xws, axis=1)
            dec_g = jnp.exp(ce_s[pl.ds(r0 + L - 1, 1), g * gw:(g + 1) * gw])
            ST[s, :, g * gw:(g + 1) * gw] = st_g * dec_g + _dot(Bm.T.astype(BF16), xw_g)
        return carry

    lax.fori_loop(0, sb * nck, chunk, 0)

    @pl.when(t == nt - 1)
    def _():
        for s in range(sb):
            sst_out_ref[s] = ST[s].T

    z = _dot(hb, wz_ref[...])
    y = y_s[...] + dsk_ref[...] * xbc_s[:, 0:di]
    y = _rms(y * (z * jax.nn.sigmoid(z)), sng_ref[...])
    yo = _dot(y.astype(BF16), wout_ref[...]).reshape(sb, tt, d)
    o_ref[...] = x + m[:, 2:3, :] * _rms(yo, ng_ref[1:2, :])


def _ssd_call(x, mods, ng, cst, sst, p, sb, tt):
    bn, T, d = x.shape
    di = p["wz"].shape[-1]
    cdim = p["wxbc"].shape[-1]
    nh = di // SSD_HEAD_DIM
    kern = functools.partial(_ssd_kernel, sb=sb, tt=tt)
    out, cst_new, sst_new = pl.pallas_call(
        kern,
        grid=(bn // sb, T // tt),
        in_specs=[
            pl.BlockSpec((sb, tt, d), lambda b, t: (b, t, 0)),
            pl.BlockSpec((sb, 6, d), lambda b, t: (b, 0, 0)),
            _const_spec((4, d)),
            pl.BlockSpec((sb, SSD_CONV - 1, cdim), lambda b, t: (b, 0, 0)),
            pl.BlockSpec((sb, di, SSD_STATE), lambda b, t: (b, 0, 0)),
            _const_spec((d, di)), _const_spec((d, cdim)), _const_spec((d, LANES)),
            _const_spec((SSD_CONV, cdim)), _const_spec((1, cdim)),
            _const_spec((1, LANES)), _const_spec((1, LANES)),
            _const_spec((1, di)), _const_spec((1, di)),
            _const_spec((di, d)), _const_spec((2 * LANES, di)),
        ],
        out_specs=[
            pl.BlockSpec((sb, tt, d), lambda b, t: (b, t, 0)),
            pl.BlockSpec((sb, SSD_CONV - 1, cdim), lambda b, t: (b, 0, 0)),
            pl.BlockSpec((sb, di, SSD_STATE), lambda b, t: (b, 0, 0)),
        ],
        out_shape=[
            jax.ShapeDtypeStruct((bn, T, d), F32),
            jax.ShapeDtypeStruct((bn, SSD_CONV - 1, cdim), F32),
            jax.ShapeDtypeStruct((bn, di, SSD_STATE), F32),
        ],
        scratch_shapes=[
            pltpu.VMEM((sb, tt + 8, cdim), F32),
            pltpu.VMEM((sb * tt, cdim), F32),
            pltpu.VMEM((sb * tt, LANES), F32),
            pltpu.VMEM((sb * tt, di), F32),
            pltpu.VMEM((sb * tt, di), F32),
            pltpu.VMEM((sb * tt, di), F32),
            pltpu.VMEM((sb, SSD_STATE, di), F32),
        ],
        compiler_params=_params(("arbitrary", "arbitrary")),
        name="ssd",
    )(x, mods, ng, cst, sst.reshape(bn, di, SSD_STATE), p["wz"], p["wxbc"], p["wdt"], p["cw"], p["cb"],
      p["dtb"], p["alog"], p["dsk"], p["sng"], p["wout"], p["e2"])
    return out, cst_new, sst_new.reshape(bn, nh, SSD_HEAD_DIM, SSD_STATE)


def _forget_lanes(f):
    nh = FOX_HEADS
    lane = lax.broadcasted_iota(jnp.int32, f.shape, 1)
    hi, mid, lo = (p.astype(F32) for p in _split3(jnp.where(lane < nh, f, 0.0)))
    parts = hi + pltpu.roll(mid, nh, 1) + pltpu.roll(lo, 2 * nh, 1)
    one = jnp.ones_like(f)
    zero = jnp.zeros_like(f)
    qa = jnp.where(lane < 3 * nh, one, jnp.where(lane < 6 * nh, pltpu.roll(parts, 3 * nh, 1), zero))
    ka = jnp.where(lane < 3 * nh, -parts, jnp.where(lane < 6 * nh, one, zero))
    return qa.astype(BF16), ka.astype(BF16)


def _fox_proj_kernel(*refs, sb, tt, past):
    if past:
        (x_ref, mod_ref, ng_ref, lfp_ref, wqkv_ref, wf_ref, bf_ref,
         q_ref, k_ref, v_ref, kb_ref, vb_ref, lf_ref, fq_ref, fp_ref, fcar) = refs
    else:
        (x_ref, mod_ref, ng_ref, wqkv_ref, wf_ref, bf_ref,
         q_ref, k_ref, v_ref, kb_ref, vb_ref, lf_ref, qa_ref, ka_ref, fcar) = refs
    t = pl.program_id(1)
    d = x_ref.shape[-1]
    nh = lf_ref.shape[-1]
    rows = sb * tt
    cblk = min(tt, 256)
    trt = jnp.where(_tril(cblk), 1.0, 0.0).astype(BF16)

    def cumsum_rows(get_rows, n, car):
        outs = []
        for r in range(0, n, cblk):
            c = _dot_exact_lhs(trt, get_rows(r, cblk)) + car
            outs.append(c)
            car = c[cblk - 1:cblk, :]
        return outs, car

    @pl.when(t == 0)
    def _():
        if past:
            plen = lfp_ref.shape[2]
            pb = 256
            triu = jnp.where(lax.broadcasted_iota(jnp.int32, (pb, pb), 0)
                             <= lax.broadcasted_iota(jnp.int32, (pb, pb), 1), 1.0, 0.0).astype(BF16)
            for s in range(sb):
                car = jnp.zeros((nh, 1), F32)
                for r in range(0, plen, pb):
                    hi, mid, lo = _split3(lfp_ref[s, :, r:r + pb])
                    c = _dot(hi, triu) + (_dot(mid, triu) + _dot(lo, triu)) + car
                    fp_ref[s, :, r:r + pb] = c * LOG2E
                    car = c[:, pb - 1:pb]
                eye = (lax.broadcasted_iota(jnp.int32, (nh, LANES), 0)
                       == lax.broadcasted_iota(jnp.int32, (nh, LANES), 1))
                tot = jnp.sum(jnp.where(eye, jnp.broadcast_to(car, (nh, LANES)), 0.0), axis=0, keepdims=True)
                fcar[s] = jnp.broadcast_to(tot, (8, LANES))
        else:
            fcar[...] = jnp.zeros((sb, 8, LANES), F32)

    x = x_ref[...]
    m = mod_ref[...]
    h = _rms(x, ng_ref[0:1, :]) * (1.0 + m[:, 1:2, :]) + m[:, 0:1, :]
    hb = h.reshape(rows, d).astype(BF16)
    qkv = _dot(hb, wqkv_ref[...])
    q = qkv[:, 0:d] * (FOX_HEAD_DIM ** -0.5 * LOG2E)
    k = qkv[:, d:2 * d]
    v = qkv[:, 2 * d:3 * d]
    q_ref[...] = q.astype(BF16).reshape(sb, tt, d)
    k_ref[...] = k.reshape(sb, tt, d)
    v_ref[...] = v.reshape(sb, tt, d)
    kb_ref[...] = k.astype(BF16).reshape(sb, tt, d)
    if past:
        vb_ref[...] = v.astype(BF16).reshape(sb, tt, d)
    else:
        for s in range(sb):
            vb_ref[s] = v[s * tt:(s + 1) * tt, :].T.astype(BF16)
    lf = jax.nn.log_sigmoid(_dot(hb, wf_ref[...]) + bf_ref[...])
    lf_ref[...] = lf[:, 0:nh].reshape(sb, tt, nh)
    for s in range(sb):
        outs, car = cumsum_rows(lambda r, n, s=s: lf[s * tt + r:s * tt + r + n, :], tt, fcar[s, 0:1, :])
        for bi, c in enumerate(outs):
            if past:
                fq_ref[s, bi * cblk:(bi + 1) * cblk, :] = c[:, 0:nh] * LOG2E
            else:
                qa, ka = _forget_lanes(c * LOG2E)
                qa_ref[s, bi * cblk:(bi + 1) * cblk, :] = qa
                ka_ref[s, bi * cblk:(bi + 1) * cblk, :] = ka
        fcar[s] = jnp.broadcast_to(car, (8, LANES))


def _fox_proj_call(x, mods, ng, lf_past, wqkv, wf, bf, sb, tt):
    bn, T, d = x.shape
    nh = FOX_HEADS
    past = lf_past is not None
    kern = functools.partial(_fox_proj_kernel, sb=sb, tt=tt, past=past)
    in_specs = [
        pl.BlockSpec((sb, tt, d), lambda b, t: (b, t, 0)),
        pl.BlockSpec((sb, 6, d), lambda b, t: (b, 0, 0)),
        _const_spec((4, d)),
    ]
    args = [x, mods, ng]
    if past:
        plen = lf_past.shape[2]
        in_specs.append(pl.BlockSpec((sb, nh, plen), lambda b, t: (b, 0, 0)))
        args.append(lf_past)
    in_specs += [_const_spec((d, 3 * d)), _const_spec((d, LANES)), _const_spec((1, LANES))]
    args += [wqkv, wf, jnp.pad(bf.reshape(1, nh), ((0, 0), (0, LANES - nh)))]
    tile = pl.BlockSpec((sb, tt, d), lambda b, t: (b, t, 0))
    small = pl.BlockSpec((sb, tt, nh), lambda b, t: (b, t, 0))
    out_specs = [tile, tile, tile, tile, tile, small, small]
    out_shape = [
        jax.ShapeDtypeStruct((bn, T, d), BF16),
        jax.ShapeDtypeStruct((bn, T, d), F32),
        jax.ShapeDtypeStruct((bn, T, d), F32),
        jax.ShapeDtypeStruct((bn, T, d), BF16),
        jax.ShapeDtypeStruct((bn, T, d), BF16),
        jax.ShapeDtypeStruct((bn, T, nh), F32),
        jax.ShapeDtypeStruct((bn, T, nh), F32),
    ]
    if past:
        out_specs.append(pl.BlockSpec((sb, nh, plen), lambda b, t: (b, 0, 0)))
        out_shape.append(jax.ShapeDtypeStruct((bn, nh, plen), F32))
    else:
        out_specs[4] = pl.BlockSpec((sb, d, tt), lambda b, t: (b, 0, t))
        out_shape[4] = jax.ShapeDtypeStruct((bn, d, T), BF16)
        wide = pl.BlockSpec((sb, tt, LANES), lambda b, t: (b, t, 0))
        out_specs[6:] = [wide, wide]
        out_shape[6:] = [jax.ShapeDtypeStruct((bn, T, LANES), BF16)] * 2
    return pl.pallas_call(
        kern,
        grid=(bn // sb, T // tt),
        in_specs=in_specs,
        out_specs=out_specs,
        out_shape=out_shape,
        scratch_shapes=[pltpu.VMEM((sb, 8, LANES), F32)],
        compiler_params=_params(("arbitrary", "arbitrary")),
        name="fox_proj",
    )(*args)


def _fox_attn_cached_kernel(x_ref, mod_ref, ng_ref, q_ref, kt_ref, vt_ref, fkt_ref, kn_ref, vn_ref, fq_ref,
                            fktn_ref, wo_ref, o_ref, acc, m_s, l_s, fq_s, *, tq):
    j = pl.program_id(1)
    nj = pl.num_programs(1)
    d = x_ref.shape[-1]
    npair = d // LANES
    hd = FOX_HEAD_DIM

    @pl.when(j == 0)
    def _():
        acc[...] = jnp.zeros(acc.shape, F32)
        l_s[...] = jnp.zeros(l_s.shape, F32)
        m_s[...] = jnp.full(m_s.shape, -1e30, F32)
        for jp in range(npair):
            fq_s[jp] = jnp.concatenate([jnp.broadcast_to(fq_ref[:, 2 * jp:2 * jp + 1], (tq, LANES)),
                                        jnp.broadcast_to(fq_ref[:, 2 * jp + 1:2 * jp + 2], (tq, LANES))], axis=0)

    low = lax.broadcasted_iota(jnp.int32, (tq, LANES), 1) < hd

    def attend(jp, s2, fk_a, fk_b, vmat, v_is_transposed, mask=None):
        nk = s2.shape[1]
        t = s2 - jnp.concatenate([jnp.broadcast_to(fk_a, (tq, nk)), jnp.broadcast_to(fk_b, (tq, nk))], axis=0)
        if mask is not None:
            t = jnp.where(mask, t, -jnp.inf)
        fq2 = fq_s[jp]
        m_old = m_s[jp]
        m_new = jnp.maximum(m_old, jnp.max(t, axis=-1, keepdims=True) + fq2)
        alpha = jnp.exp2(m_old - m_new)
        p = jnp.exp2(t - (m_new - fq2)[:, 0:1])
        l_s[jp] = alpha * l_s[jp] + jnp.sum(p, axis=-1, keepdims=True)
        m_s[jp] = m_new
        pb = p.astype(BF16)
        pv = _dot_nt(pb, vmat) if v_is_transposed else _dot(pb, vmat)
        acc[jp] = alpha * acc[jp] + pv

    def q_stack(jp):
        q2 = q_ref[:, jp * LANES:(jp + 1) * LANES]
        zq = jnp.zeros_like(q2)
        return jnp.concatenate([jnp.where(low, q2, zq), jnp.where(low, zq, q2)], axis=0)

    for jp in range(npair):
        kt2 = kt_ref[jp * LANES:(jp + 1) * LANES, :].astype(BF16)
        vt2 = vt_ref[jp * LANES:(jp + 1) * LANES, :].astype(BF16)
        attend(jp, _dot(q_stack(jp), kt2), fkt_ref[2 * jp:2 * jp + 1, :], fkt_ref[2 * jp + 1:2 * jp + 2, :],
               vt2, True)

    @pl.when(j == nj - 1)
    def _():
        rowq = lax.broadcasted_iota(jnp.int32, (2 * tq, LANES), 0)
        rowq = jnp.where(rowq >= tq, rowq - tq, rowq)
        causal = lax.broadcasted_iota(jnp.int32, (2 * tq, LANES), 1) <= rowq
        zpad = jnp.zeros((LANES - tq, LANES), BF16)
        for jp in range(npair):
            kn2 = jnp.concatenate([kn_ref[:, jp * LANES:(jp + 1) * LANES], zpad], axis=0)
            vn2 = jnp.concatenate([vn_ref[:, jp * LANES:(jp + 1) * LANES], zpad], axis=0)
            attend(jp, _dot_nt(q_stack(jp), kn2), fktn_ref[2 * jp:2 * jp + 1, :],
                   fktn_ref[2 * jp + 1:2 * jp + 2, :], vn2, False, causal)
        cols = []
        for jp in range(npair):
            a, l = acc[jp], l_s[jp]
            cols.append((jnp.where(low, a[0:tq], a[tq:2 * tq]) / jnp.where(low, l[0:tq], l[tq:2 * tq])).astype(BF16))
        y = _dot(jnp.concatenate(cols, axis=1), wo_ref[...])
        m = mod_ref[...]
        o_ref[...] = x_ref[...] + m[2:3, :] * _rms(y, ng_ref[1:2, :])


def _fox_attn_cached_call(x, mods, ng, q, kt_past, vt_past, fkt_past, kb_new, vb_new, fq, fkt_new, wo):
    bn, tq, d = x.shape
    plen = kt_past.shape[-1]
    nh = FOX_HEADS
    assert tq <= LANES and plen % 1024 == 0
    tkb = 1024
    kern = functools.partial(_fox_attn_cached_kernel, tq=tq)
    row = pl.BlockSpec((None, tq, d), lambda b, j: (b, 0, 0))
    return pl.pallas_call(
        kern,
        grid=(bn, plen // tkb),
        in_specs=[
            row,
            pl.BlockSpec((None, 6, d), lambda b, j: (b, 0, 0)),
            _const_spec((4, d)),
            row,
            pl.BlockSpec((None, d, tkb), lambda b, j: (b, 0, j)),
            pl.BlockSpec((None, d, tkb), lambda b, j: (b, 0, j)),
            pl.BlockSpec((None, nh, tkb), lambda b, j: (b, 0, j)),
            row,
            row,
            pl.BlockSpec((None, tq, nh), lambda b, j: (b, 0, 0)),
            pl.BlockSpec((None, nh, LANES), lambda b, j: (b, 0, 0)),
            _const_spec((d, d)),
        ],
        out_specs=row,
        out_shape=jax.ShapeDtypeStruct((bn, tq, d), F32),
        scratch_shapes=[
            pltpu.VMEM((d // LANES, 2 * tq, LANES), F32),
            pltpu.VMEM((d // LANES, 2 * tq, LANES), F32),
            pltpu.VMEM((d // LANES, 2 * tq, LANES), F32),
            pltpu.VMEM((d // LANES, 2 * tq, LANES), F32),
        ],
        compiler_params=_params(("arbitrary", "arbitrary")),
        name="fox_attn_cached",
    )(x, mods, ng, q, kt_past, vt_past, fkt_past, kb_new, vb_new, fq, fkt_new, wo)


def _fox_attn_t_kernel(x_ref, mod_ref, ng_ref, q_ref, k_ref, vt_ref, qa_ref, ka_ref, wo_ref,
                       o_ref, acct, m_s, *, tq, tk, rs):
    qi = pl.program_id(1)
    ki = pl.program_id(2)
    nk = pl.num_programs(2)
    d = x_ref.shape[-1]
    npair = d // LANES
    hd = FOX_HEAD_DIM
    nstrip = tk // rs
    nqb = tq // LANES

    @pl.when(ki == 0)
    def _():
        acct[...] = jnp.zeros(acct.shape, F32)
        m_s[...] = jnp.full(m_s.shape, -1e30, F32)

    lane_q = lax.broadcasted_iota(jnp.int32, (tq, LANES), 1)
    low = lane_q < hd
    ones_v = jnp.ones((hd, tk), BF16)

    def step(masked):
        if masked:
            dk = (lax.broadcasted_iota(jnp.int32, (rs, LANES), 0)
                  - lax.broadcasted_iota(jnp.int32, (rs, LANES), 1))
        qa = qa_ref[...]
        ka = ka_ref[...]

        def block_kind(i, c):
            if not masked:
                return 2
            if i * rs > c * LANES + LANES - 1:
                return 0
            if i * rs + rs - 1 <= c * LANES:
                return 2
            return 1

        def logits(st, i, c):
            t = st[i * rs:(i + 1) * rs, c * LANES:(c + 1) * LANES]
            if block_kind(i, c) == 1:
                t = jnp.where(dk <= (c * LANES - i * rs), t, -jnp.inf)
            return t

        def qk(hh):
            j, half = hh // 2, hh % 2
            q2 = q_ref[:, j * LANES:(j + 1) * LANES]
            qh = jnp.where(low if half == 0 else ~low, q2, jnp.zeros_like(q2))
            sel = ((lane_q & (FOX_HEADS - 1)) == hh) & (lane_q < 6 * FOX_HEADS)
            qcat = jnp.concatenate([qh, jnp.where(sel, qa, jnp.zeros_like(qa))], axis=1)
            kcat = jnp.concatenate([k_ref[:, j * LANES:(j + 1) * LANES], ka], axis=1)
            return _dot_nt(kcat, qcat)

        ahead = 2
        sts = [qk(h0) for h0 in range(ahead)]
        for hh in range(2 * npair):
            j, half = hh // 2, hh % 2
            st = sts.pop(0)
            if hh + ahead < 2 * npair:
                sts.append(qk(hh + ahead))
            vt2 = vt_ref[j * LANES:(j + 1) * LANES, :]
            m_old = m_s[hh:hh + 1, :]
            pm = [None] * nqb
            for i in range(nstrip):
                for c in range(nqb):
                    if block_kind(i, c) > 0:
                        t = logits(st, i, c)
                        pm[c] = t if pm[c] is None else jnp.maximum(pm[c], t)
            mx = jnp.concatenate([jnp.max(p_, axis=0, keepdims=True) for p_ in pm], axis=1)
            m_new = jnp.maximum(m_old, mx)
            alpha = jnp.exp2(m_old - m_new)
            prow = []
            for i in range(nstrip):
                pieces = []
                for c in range(nqb):
                    if block_kind(i, c) > 0:
                        pieces.append(jnp.exp2(logits(st, i, c)
                                               - m_new[:, c * LANES:(c + 1) * LANES]).astype(BF16))
                    else:
                        pieces.append(jnp.zeros((rs, LANES), BF16))
                prow.append(jnp.concatenate(pieces, axis=1))
            vaug = jnp.concatenate([vt2[half * hd:(half + 1) * hd, :], ones_v], axis=0)
            ot = _dot(vaug, jnp.concatenate(prow, axis=0))
            acct[hh] = alpha * acct[hh] + ot
            m_s[hh:hh + 1, :] = m_new

    @pl.when(ki == qi)
    def _():
        step(True)

    @pl.when(ki < qi)
    def _():
        step(False)

    @pl.when(ki == nk - 1)
    def _():
        ots = []
        for hh in range(2 * npair):
            a = acct[hh]
            ots.append(a[0:hd] / a[hd:hd + 1])
        o = jnp.concatenate(ots, axis=0).T.astype(BF16)
        y = _dot(o, wo_ref[...])
        m = mod_ref[...]
        o_ref[...] = x_ref[...] + m[2:3, :] * _rms(y, ng_ref[1:2, :])


def _fox_attn_t_call(x, mods, ng, q, kb, vt, qa, ka, wo, tq):
    bn, T, d = x.shape
    nh = FOX_HEADS
    tk = tq
    kern = functools.partial(_fox_attn_t_kernel, tq=tq, tk=tk, rs=32)
    return pl.pallas_call(
        kern,
        grid=(bn, T // tq, T // tk),
        in_specs=[
            pl.BlockSpec((None, tq, d), lambda b, i, j: (b, i, 0)),
            pl.BlockSpec((None, 6, d), lambda b, i, j: (b, 0, 0)),
            _const_spec((4, d)),
            pl.BlockSpec((None, tq, d), lambda b, i, j: (b, i, 0)),
            pl.BlockSpec((None, tk, d), lambda b, i, j: (b, jnp.minimum(j, i), 0)),
            pl.BlockSpec((None, d, tk), lambda b, i, j: (b, 0, jnp.minimum(j, i))),
            pl.BlockSpec((None, tq, LANES), lambda b, i, j: (b, i, 0)),
            pl.BlockSpec((None, tk, LANES), lambda b, i, j: (b, jnp.minimum(j, i), 0)),
            _const_spec((d, d)),
        ],
        out_specs=pl.BlockSpec((None, tq, d), lambda b, i, j: (b, i, 0)),
        out_shape=jax.ShapeDtypeStruct((bn, T, d), F32),
        scratch_shapes=[
            pltpu.VMEM((nh, LANES, tq), F32),
            pltpu.VMEM((nh, tq), F32),
        ],
        compiler_params=_params(("arbitrary", "arbitrary", "arbitrary")),
        name="fox_attn_t",
    )(x, mods, ng, q, kb, vt, qa, ka, wo)


def _prep_weights(W):
    bf = lambda a: a.astype(BF16)
    di = W["ssd_w_out"].shape[0]
    nh = di // SSD_HEAD_DIM
    cdim = W["ssd_conv_w"].shape[-1]
    d = W["ssd_w_in"].shape[0]
    lru = []
    for j in range(W["lru_w_x"].shape[0]):
        lru.append(dict(
            wx=bf(W["lru_w_x"][j]), bx=W["lru_b_x"][j], wy=bf(W["lru_w_y"][j]), by=W["lru_b_y"][j],
            cw=W["lru_conv_w"][j], cb=W["lru_conv_b"][j],
            wai=bf(0.5 * jnp.concatenate([W["lru_w_a"][j], W["lru_w_i"][j]], axis=-1)),
            ba=0.5 * W["lru_b_a"][j], bi=0.5 * W["lru_b_i"][j], lam=W["lru_lambda"][j],
            wo=bf(W["lru_w_o"][j]), bo=W["lru_b_o"][j]))
    w_in = W["ssd_w_in"]
    pad_l = lambda v: jnp.pad(v.reshape(1, -1), ((0, 0), (0, LANES - v.shape[-1])))
    onehot = (jnp.arange(LANES)[:, None] == jnp.arange(di)[None, :] // SSD_HEAD_DIM).astype(BF16)
    ssd = dict(
        e2=jnp.concatenate([onehot, onehot], axis=0),
        wz=bf(w_in[:, :di]), wxbc=bf(w_in[:, di:di + cdim]),
        wdt=bf(jnp.pad(w_in[:, di + cdim:], ((0, 0), (0, LANES - nh)))),
        cw=W["ssd_conv_w"], cb=W["ssd_conv_b"].reshape(1, cdim),
        dtb=pad_l(W["ssd_dt_bias"]), alog=pad_l(W["ssd_a_log"]),
        dsk=jnp.repeat(W["ssd_d"], SSD_HEAD_DIM).reshape(1, di), sng=W["ssd_norm_g"].reshape(1, di),
        wout=bf(W["ssd_w_out"]))
    fox = dict(
        wqkv=bf(W["fox_w_qkv"]),
        wf=bf(jnp.pad(W["fox_w_f"], ((0, 0), (0, LANES - W["fox_w_f"].shape[-1])))),
        bf=W["fox_b_f"], wo=bf(W["fox_w_o"]))
    ffn = dict(wg=bf(W["ffn_w_gate"]), wu=bf(W["ffn_w_up"]), wd=bf(W["ffn_w_down"]),
               cw=W["ffn_conv_w"], cb=W["ffn_conv_b"])
    return dict(lru=lru, ssd=ssd, fox=fox, ffn=ffn, norm_g=W["norm_g"])


def _run_trunk(x, mods, prev, P, cfg):
    bn, T, d = x.shape
    sb, tt, tt_ffn, tq, tk = cfg["sb"], cfg["tt"], cfg["tt_ffn"], cfg["tq"], cfg["tk"]
    depth = mods.shape[0]
    new = {}
    ffn_bufs = []
    for layer in range(depth):
        ng = P["norm_g"][layer]
        ml = mods[layer]
        kind = layer % 3
        tag = "l%d" % layer
        if kind == 0:
            x, cn, hn = _lru_call(x, ml, ng, prev[tag + "_conv"], prev[tag + "_h"], P["lru"][layer // 3], sb, tt)
            new[tag + "_conv"], new[tag + "_h"] = cn, hn
        elif kind == 1:
            x, cn, sn = _ssd_call(x, ml, ng, prev[tag + "_conv"], prev[tag + "_ssm"], P["ssd"], cfg["sb_ssd"], tt)
            new[tag + "_conv"], new[tag + "_ssm"] = cn, sn
        else:
            fx = P["fox"]
            kp, vp, lp = prev.get(tag + "_k"), prev.get(tag + "_v"), prev.get(tag + "_logf")
            lpt = None if lp is None else jnp.swapaxes(lp, 1, 2)
            res = _fox_proj_call(x, ml, ng, lpt, fx["wqkv"], fx["wf"], fx["bf"], sb, tt)
            q, k, v, kb, vb, lf, fq = res[:7]
            if kp is None:
                x = _fox_attn_t_call(x, ml, ng, q, kb, vb, res[6], res[7], fx["wo"], tq)
            else:
                plen = kp.shape[1]
                kt_past = jnp.transpose(kp, (0, 2, 3, 1)).reshape(bn, d, plen)
                vt_past = jnp.transpose(vp, (0, 2, 3, 1)).reshape(bn, d, plen)
                fkt_new = jnp.pad(jnp.swapaxes(fq, 1, 2), ((0, 0), (0, 0), (0, LANES - T)))
                x = _fox_attn_cached_call(x, ml, ng, q, kt_past, vt_past, res[7], kb, vb, fq, fkt_new, fx["wo"])
            new[tag + "_k"] = k.reshape(bn, T, FOX_HEADS, FOX_HEAD_DIM)
            new[tag + "_v"] = v.reshape(bn, T, FOX_HEADS, FOX_HEAD_DIM)
            new[tag + "_logf"] = lf
        f = P["ffn"]
        x, buf = _ffn_call(x, ml, ng, prev["ffn_conv"][layer], f["wg"], f["wu"], f["cw"][layer], f["cb"][layer],
                           f["wd"], layer, sb, tt_ffn)
        ffn_bufs.append(buf)
    new["ffn_conv"] = jnp.stack(ffn_bufs)
    return x, new


def _fresh_state(bn, W):
    dr = W["lru_w_x"].shape[-1]
    cdim = W["ssd_conv_w"].shape[-1]
    di = W["ssd_w_out"].shape[0]
    f = W["ffn_w_gate"].shape[-1]
    depth = W["ffn_w_gate"].shape[0]
    st = dict(ffn_conv=jnp.zeros((depth, bn, FFN_CONV - 1, f), F32))
    for layer in range(depth):
        tag = "l%d" % layer
        if layer % 3 == 0:
            st[tag + "_conv"] = jnp.zeros((bn, LRU_CONV - 1, dr), F32)
            st[tag + "_h"] = jnp.zeros((bn, dr), F32)
        elif layer % 3 == 1:
            st[tag + "_conv"] = jnp.zeros((bn, SSD_CONV - 1, cdim), F32)
            st[tag + "_ssm"] = jnp.zeros((bn, di // SSD_HEAD_DIM, SSD_HEAD_DIM, SSD_STATE), F32)
    return st


def _tile_cfg(bn, T):
    if T >= 256:
        return dict(sb=1, sb_ssd=1, tt=256, tt_ffn=256, tq=512 if T % 512 == 0 else 256,
                    tk=512 if T % 512 == 0 else 256)
    sb = 4 if bn % 4 == 0 else 1
    return dict(sb=sb, sb_ssd=2 if bn % 2 == 0 else 1, tt=T, tt_ffn=T, tq=T, tk=128)


def kernel(x_prompt, x_sample, c_prompt, c_sample, state_l0_conv, state_l0_h, state_l1_conv, state_l1_ssm, cache_l2_k, cache_l2_v, cache_l2_logf, state_l3_conv, state_l3_h, state_ffn_conv, ada_w, ada_b, norm_g, lru_w_x, lru_b_x, lru_w_y, lru_b_y, lru_conv_w, lru_conv_b, lru_w_a, lru_b_a, lru_w_i, lru_b_i, lru_lambda, lru_w_o, lru_b_o, ssd_w_in, ssd_conv_w, ssd_conv_b, ssd_dt_bias, ssd_a_log, ssd_d, ssd_norm_g, ssd_w_out, fox_w_qkv, fox_w_f, fox_b_f, fox_w_o, ffn_w_gate, ffn_w_up, ffn_conv_w, ffn_conv_b, ffn_w_down):
    W = dict(ada_w=ada_w, ada_b=ada_b, norm_g=norm_g,
             lru_w_x=lru_w_x, lru_b_x=lru_b_x, lru_w_y=lru_w_y, lru_b_y=lru_b_y,
             lru_conv_w=lru_conv_w, lru_conv_b=lru_conv_b, lru_w_a=lru_w_a, lru_b_a=lru_b_a,
             lru_w_i=lru_w_i, lru_b_i=lru_b_i, lru_lambda=lru_lambda, lru_w_o=lru_w_o, lru_b_o=lru_b_o,
             ssd_w_in=ssd_w_in, ssd_conv_w=ssd_conv_w, ssd_conv_b=ssd_conv_b, ssd_dt_bias=ssd_dt_bias,
             ssd_a_log=ssd_a_log, ssd_d=ssd_d, ssd_norm_g=ssd_norm_g, ssd_w_out=ssd_w_out,
             fox_w_qkv=fox_w_qkv, fox_w_f=fox_w_f, fox_b_f=fox_b_f, fox_w_o=fox_w_o,
             ffn_w_gate=ffn_w_gate, ffn_w_up=ffn_w_up, ffn_conv_w=ffn_conv_w, ffn_conv_b=ffn_conv_b,
             ffn_w_down=ffn_w_down)
    P = _prep_weights(W)
    bp, bs = x_prompt.shape[0], x_sample.shape[0]
    d = x_prompt.shape[-1]
    depth = ada_w.shape[0]
    mods = _ada_call(jnp.concatenate([c_prompt, c_sample], axis=0), ada_w, ada_b)
    mods = mods.reshape(depth, bp + bs, 6, d)

    y_prompt, p = _run_trunk(x_prompt, mods[:, :bp], _fresh_state(bp, W), P, _tile_cfg(bp, x_prompt.shape[1]))
    prev = dict(l0_conv=state_l0_conv, l0_h=state_l0_h, l1_conv=state_l1_conv, l1_ssm=state_l1_ssm,
                l2_k=cache_l2_k, l2_v=cache_l2_v, l2_logf=cache_l2_logf,
                l3_conv=state_l3_conv, l3_h=state_l3_h, ffn_conv=state_ffn_conv)
    y_sample, s = _run_trunk(x_sample, mods[:, bp:], prev, P, _tile_cfg(bs, x_sample.shape[1]))
    return (y_prompt, y_sample,
            p['l0_conv'], p['l0_h'], p['l1_conv'], p['l1_ssm'], p['l2_k'], p['l2_v'], p['l2_logf'],
            p['l3_conv'], p['l3_h'], p['ffn_conv'],
            s['l0_conv'], s['l0_h'], s['l1_conv'], s['l1_ssm'], s['l2_k'], s['l2_v'], s['l2_logf'],
            s['l3_conv'], s['l3_h'], s['ffn_conv'])
```

```python
import functools

import jax
import jax.numpy as jnp
from jax import lax
from jax.experimental import pallas as pl
from jax.experimental.pallas import tpu as pltpu

F32 = jnp.float32
BF16 = jnp.bfloat16

EPS = 1e-6
LOG2E = 1.4426950408889634
LRU_C = 8.0
LRU_BLOCKS = 8
LRU_CONV = 4
SSD_HEAD_DIM = 64
SSD_GROUPS = 4
SSD_STATE = 128
SSD_CONV = 4
SSD_CHUNK = 64
FOX_HEADS = 16
FOX_HEAD_DIM = 64
FFN_CONV = 3

LANES = 128
SUBLANES = 8
VMEM_LIMIT = 56 * 1024 * 1024


def _const_spec(shape):
    nd = len(shape)
    return pl.BlockSpec(shape, lambda *_: (0,) * nd, pipeline_mode=pl.Buffered(1))


def _params(sem):
    return pltpu.CompilerParams(dimension_semantics=sem, vmem_limit_bytes=VMEM_LIMIT)


def _rms(x, g):
    return x * lax.rsqrt(jnp.mean(x * x, axis=-1, keepdims=True) + EPS) * g


def _dot(a, b):
    return jnp.dot(a, b, preferred_element_type=F32)


def _dot_nt(a, b):
    return lax.dot_general(a, b, (((1,), (1,)), ((), ())), preferred_element_type=F32)


def _dot_tn(a, b):
    return lax.dot_general(a, b, (((0,), (0,)), ((), ())), preferred_element_type=F32)


def _split3(x):
    hi = x.astype(BF16)
    r1 = x - hi.astype(F32)
    mid = r1.astype(BF16)
    lo = (r1 - mid.astype(F32)).astype(BF16)
    return hi, mid, lo


def _dot_exact_lhs(m, x):
    hi, mid, lo = _split3(x)
    return _dot(m, hi) + (_dot(m, mid) + _dot(m, lo))


def _expm1_nonpos(x):
    u = jnp.exp(x)
    small = (u - 1.0) * x / jnp.log(u)
    return jnp.where(x < -1.0, u - 1.0, jnp.where(u == 1.0, x, small))


def _gelu_tanh(x):
    c0 = 0.7978845608028654
    hx = 0.5 * x
    return hx + hx * jnp.tanh(x * (c0 + (c0 * 0.044715) * (x * x)))


def _tril(n):
    r = lax.broadcasted_iota(jnp.int32, (n, n), 0)
    c = lax.broadcasted_iota(jnp.int32, (n, n), 1)
    return r >= c


def _ada_kernel(c_ref, w_ref, b_ref, o_ref):
    c = c_ref[...]
    s = (c * jax.nn.sigmoid(c)).astype(BF16)
    o_ref[...] = _dot(s, w_ref[...].astype(BF16)) + b_ref[...]


def _ada_call(c_all, ada_w, ada_b):
    depth, d, n = ada_w.shape
    bn = c_all.shape[0]
    tn = 1536 if n % 1536 == 0 else n
    return pl.pallas_call(
        _ada_kernel,
        grid=(depth, n // tn),
        in_specs=[
            pl.BlockSpec((bn, d), lambda l, j: (0, 0)),
            pl.BlockSpec((None, d, tn), lambda l, j: (l, 0, j)),
            pl.BlockSpec((None, 1, tn), lambda l, j: (l, 0, j)),
        ],
        out_specs=pl.BlockSpec((None, bn, tn), lambda l, j: (l, 0, j)),
        out_shape=jax.ShapeDtypeStruct((depth, bn, n), F32),
        compiler_params=_params(("arbitrary", "arbitrary")),
        name="ada",
    )(c_all, ada_w, ada_b.reshape(depth, 1, n))


def _conv_from_buf(buf, cur, w_ref, b_ref, width, tt):
    y = b_ref[...] + w_ref[width - 1:width, :] * cur
    for j in range(1, width):
        y = y + w_ref[width - 1 - j:width - j, :] * buf[:, 8 - j:8 - j + tt, :]
    return y


def _ffn_kernel(x_ref, mod_ref, ng_ref, st_ref, wg_ref, wu_ref, cw_ref, cb_ref, wd_ref,
                o_ref, st_out_ref, buf, *, sb, tt):
    t = pl.program_id(1)
    d = x_ref.shape[-1]
    f = wg_ref.shape[-1]
    w = FFN_CONV

    @pl.when(t == 0)
    def _():
        buf[:, 0:8, :] = jnp.zeros((sb, 8, f), F32)
        buf[:, 8 - (w - 1):8, :] = st_ref[...]

    x = x_ref[...]
    m = mod_ref[...]
    h = _rms(x, ng_ref[2:3, :]) * (1.0 + m[:, 4:5, :]) + m[:, 3:4, :]
    hb = h.reshape(sb * tt, d).astype(BF16)
    gp = _dot(hb, wg_ref[...]).reshape(sb, tt, f)
    up = _dot(hb, wu_ref[...])
    buf[:, 8:8 + tt, :] = gp
    g = _conv_from_buf(buf, gp, cw_ref, cb_ref, w, tt)
    st_out_ref[...] = buf[:, 8 + tt - (w - 1):8 + tt, :]
    buf[:, 0:8, :] = buf[:, tt:tt + 8, :]
    act = (_gelu_tanh(g).reshape(sb * tt, f) * up).astype(BF16)
    y = _dot(act, wd_ref[...]).reshape(sb, tt, d)
    o_ref[...] = x + m[:, 5:6, :] * _rms(y, ng_ref[3:4, :])


def _ffn_call(x, mods, ng, st, wg, wu, cw, cb, wd, layer, sb, tt):
    bn, T, d = x.shape
    f = wg.shape[-1]
    kern = functools.partial(_ffn_kernel, sb=sb, tt=tt)

    def layer_spec(shape):
        return pl.BlockSpec((None,) + shape, lambda b, t: (layer, 0, 0), pipeline_mode=pl.Buffered(1))

    return pl.pallas_call(
        kern,
        grid=(bn // sb, T // tt),
        in_specs=[
            pl.BlockSpec((sb, tt, d), lambda b, t: (b, t, 0)),
            pl.BlockSpec((sb, 6, d), lambda b, t: (b, 0, 0)),
            _const_spec((4, d)),
            pl.BlockSpec((sb, FFN_CONV - 1, f), lambda b, t: (b, 0, 0)),
            layer_spec((d, f)),
            layer_spec((d, f)),
            _const_spec((FFN_CONV, f)),
            _const_spec((1, f)),
            layer_spec((f, d)),
        ],
        out_specs=[
            pl.BlockSpec((sb, tt, d), lambda b, t: (b, t, 0)),
            pl.BlockSpec((sb, FFN_CONV - 1, f), lambda b, t: (b, 0, 0)),
        ],
        out_shape=[
            jax.ShapeDtypeStruct((bn, T, d), F32),
            jax.ShapeDtypeStruct((bn, FFN_CONV - 1, f), F32),
        ],
        scratch_shapes=[pltpu.VMEM((sb, tt + 8, f), F32)],
        compiler_params=_params(("arbitrary", "arbitrary")),
        name="ffn",
    )(x, mods, ng, st, wg, wu, cw, cb.reshape(1, f), wd)


def _lru_pitch(tt):
    seg = tt // 8
    assert seg % 8 == 0
    return seg if (seg // 8) % 2 == 1 else seg + 8


def _lru_kernel(x_ref, mod_ref, ng_ref, cst_ref, hst_ref, wx_ref, bx_ref, wy_ref, by_ref,
                cw_ref, cb_ref, wai_ref, ba_ref, bi_ref, lam_ref, wo_ref, bo_ref,
                o_ref, cst_out_ref, hst_out_ref, buf, za_s, zi_s, u_s, g_s, hcar, *, sb, tt):
    t = pl.program_id(1)
    d = x_ref.shape[-1]
    dr = wx_ref.shape[-1]
    bw = dr // LRU_BLOCKS
    w = LRU_CONV
    rows = sb * tt

    @pl.when(t == 0)
    def _():
        buf[:, 0:8, :] = jnp.zeros((sb, 8, dr), F32)
        buf[:, 8 - (w - 1):8, :] = cst_ref[...]
        hcar[...] = jnp.broadcast_to(hst_ref[...], (sb, 8, dr))

    seg = tt // 8
    pitch = _lru_pitch(tt)
    nsl = dr // LANES

    def stage(ref, val, n):
        for s in range(sb):
            for k in range(8):
                r = (s * 8 + k) * pitch
                ref[n, r:r + seg, :] = val[s * tt + k * seg:s * tt + (k + 1) * seg, :]

    khalf = (-0.5 * LRU_C) * jax.nn.softplus(-lam_ref[...])
    row8 = lax.broadcasted_iota(jnp.int32, (8, LANES), 0)

    def scan_slabs(lo, hi):
        for s in range(sb):
            for n in range(lo, hi):
                ls = slice(n * LANES, (n + 1) * LANES)
                kh = khalf[:, ls]

                def rows_of(g, s=s):
                    return pl.ds(s * 8 * pitch + g, 8, stride=pitch)

                hloc = jnp.zeros((8, LANES), F32)
                prod = jnp.ones((8, LANES), F32)
                hlocs, prods = [], []
                for g in range(seg):
                    log_a = kh + kh * jnp.tanh(za_s[n, rows_of(g), :] + ba_ref[:, ls])
                    hu = 0.5 * u_s[n, rows_of(g), :]
                    iu = hu + hu * jnp.tanh(zi_s[n, rows_of(g), :] + bi_ref[:, ls])
                    a = jnp.exp(log_a)
                    hloc = a * hloc + jnp.sqrt(jnp.tanh(-log_a) * (1.0 + a * a)) * iu
                    prod = a * prod
                    hlocs.append(hloc)
                    prods.append(prod)
                h0 = hcar[s, :, ls]
                hin = h0
                for _ in range(7):
                    hin = jnp.where(row8 == 0, h0, pltpu.roll(hloc + prod * hin, 1, 0))
                hcar[s, :, ls] = jnp.broadcast_to((hloc + prod * hin)[7:8, :], (8, LANES))
                for g in range(seg):
                    g_s[n, rows_of(g), :] = (hlocs[g] + prods[g] * hin) * _gelu_tanh(g_s[n, rows_of(g), :])

    x = x_ref[...]
    m = mod_ref[...]
    h = _rms(x, ng_ref[0:1, :]) * (1.0 + m[:, 1:2, :]) + m[:, 0:1, :]
    hb = h.reshape(rows, d).astype(BF16)
    xx = (_dot(hb, wx_ref[...]) + bx_ref[...]).reshape(sb, tt, dr)
    buf[:, 8:8 + tt, :] = xx
    u = _conv_from_buf(buf, xx, cw_ref, cb_ref, w, tt).reshape(rows, dr)
    cst_out_ref[...] = buf[:, 8 + tt - (w - 1):8 + tt, :]
    buf[:, 0:8, :] = buf[:, tt:tt + 8, :]
    gp = _dot(hb, wy_ref[...]) + by_ref[...]
    ub = u.astype(BF16)
    for n in range(LRU_BLOCKS):
        z = _dot(ub[:, n * bw:(n + 1) * bw], wai_ref[n])
        stage(za_s, z[:, :bw], n)
        stage(zi_s, z[:, bw:], n)
        stage(u_s, u[:, n * bw:(n + 1) * bw], n)
        stage(g_s, gp[:, n * bw:(n + 1) * bw], n)
    scan_slabs(0, nsl)
    hst_out_ref[...] = hcar[:, 7:8, :]

    cols = []
    for n in range(nsl):
        cols.append(jnp.concatenate([g_s[n, (s * 8 + k) * pitch:(s * 8 + k) * pitch + seg, :]
                                     for s in range(sb) for k in range(8)], axis=0))
    y = (_dot(jnp.concatenate(cols, axis=1).astype(BF16), wo_ref[...]) + bo_ref[...]).reshape(sb, tt, d)
    o_ref[...] = x + m[:, 2:3, :] * _rms(y, ng_ref[1:2, :])


def _lru_call(x, mods, ng, cst, hst, p, sb, tt):
    bn, T, d = x.shape
    dr = p["wx"].shape[-1]
    bw = dr // LRU_BLOCKS
    kern = functools.partial(_lru_kernel, sb=sb, tt=tt)
    row = lambda v: v.reshape(1, -1)
    slab = pltpu.VMEM((dr // LANES, sb * 8 * _lru_pitch(tt), LANES), F32)
    out, cst_new, hst_new = pl.pallas_call(
        kern,
        grid=(bn // sb, T // tt),
        in_specs=[
            pl.BlockSpec((sb, tt, d), lambda b, t: (b, t, 0)),
            pl.BlockSpec((sb, 6, d), lambda b, t: (b, 0, 0)),
            _const_spec((4, d)),
            pl.BlockSpec((sb, LRU_CONV - 1, dr), lambda b, t: (b, 0, 0)),
            pl.BlockSpec((sb, 1, dr), lambda b, t: (b, 0, 0)),
            _const_spec((d, dr)), _const_spec((1, dr)),
            _const_spec((d, dr)), _const_spec((1, dr)),
            _const_spec((LRU_CONV, dr)), _const_spec((1, dr)),
            _const_spec((LRU_BLOCKS, bw, 2 * bw)), _const_spec((1, dr)), _const_spec((1, dr)),
            _const_spec((1, dr)),
            _const_spec((dr, d)), _const_spec((1, d)),
        ],
        out_specs=[
            pl.BlockSpec((sb, tt, d), lambda b, t: (b, t, 0)),
            pl.BlockSpec((sb, LRU_CONV - 1, dr), lambda b, t: (b, 0, 0)),
            pl.BlockSpec((sb, 1, dr), lambda b, t: (b, 0, 0)),
        ],
        out_shape=[
            jax.ShapeDtypeStruct((bn, T, d), F32),
            jax.ShapeDtypeStruct((bn, LRU_CONV - 1, dr), F32),
            jax.ShapeDtypeStruct((bn, 1, dr), F32),
        ],
        scratch_shapes=[pltpu.VMEM((sb, tt + 8, dr), F32), slab, slab, slab, slab, pltpu.VMEM((sb, 8, dr), F32)],
        compiler_params=_params(("arbitrary", "arbitrary")),
        name="lru",
    )(x, mods, ng, cst, hst.reshape(bn, 1, dr), p["wx"], row(p["bx"]), p["wy"], row(p["by"]),
      p["cw"], row(p["cb"]), p["wai"], row(p["ba"]), row(p["bi"]), row(p["lam"]),
      p["wo"], row(p["bo"]))
    return out, cst_new, hst_new.reshape(bn, dr)


def _ssd_kernel(x_ref, mod_ref, ng_ref, cst_ref, sst_ref, wz_ref, wxbc_ref, wdt_ref, cw_ref, cb_ref,
                dtb_ref, alog_ref, dsk_ref, sng_ref, wout_ref, e2_ref,
                o_ref, cst_out_ref, sst_out_ref, buf, xbc_s, cum_s, ce_s, dte_s, y_s, ST, *, sb, tt):
    t = pl.program_id(1)
    nt = pl.num_programs(1)
    d = x_ref.shape[-1]
    di = wz_ref.shape[-1]
    cdim = wxbc_ref.shape[-1]
    w = SSD_CONV
    L = SSD_CHUNK
    P = SSD_HEAD_DIM
    N = SSD_STATE
    G = SSD_GROUPS
    hpg = di // P // G
    rows = sb * tt

    @pl.when(t == 0)
    def _():
        buf[:, 0:8, :] = jnp.zeros((sb, 8, cdim), F32)
        buf[:, 8 - (w - 1):8, :] = cst_ref[...]
        for s in range(sb):
            ST[s] = sst_ref[s].T

    nck = tt // L
    lane2 = lax.broadcasted_iota(jnp.int32, (L, LANES), 1)
    low_b = lane2 < P
    tril2 = lax.broadcasted_iota(jnp.int32, (L, LANES), 0) >= (lane2 & (P - 1))
    gw = hpg * P

    def chunk(idx):
        s = idx // nck
        r0 = idx * L
        rws = slice(r0, r0 + L)
        xbc_t, ce_t, dte_t = xbc_s, ce_s, dte_s
        cum_t = cum_s[rws, :].T
        for g in range(G):
            Bm = xbc_t[rws, di + g * N:di + (g + 1) * N]
            Cm = xbc_t[rws, di + G * N + g * N:di + G * N + (g + 1) * N]
            Bb = Bm.astype(BF16)
            Cb = Cm.astype(BF16)
            cb2 = _dot_nt(Cb, jnp.concatenate([Bb, Bb], axis=0))
            st_g = ST[s, :, g * gw:(g + 1) * gw]
            yo_g = _dot(Cb, st_g.astype(BF16))
            xws = []
            for pr in range(hpg // 2):
                j = g * (hpg // 2) + pr
                sl = slice(j * LANES, (j + 1) * LANES)
                ce2 = ce_t[rws, sl]
                xdt2 = xbc_t[rws, sl] * dte_t[rws, sl]
                crow2 = jnp.concatenate([cum_t[2 * j:2 * j + 1, :], cum_t[2 * j + 1:2 * j + 2, :]], axis=1)
                m2 = (cb2 * jnp.exp(jnp.where(tril2, ce2 - crow2, -jnp.inf))).astype(BF16)
                xb = xdt2.astype(BF16)
                zb = jnp.zeros_like(xb)
                rhs = jnp.concatenate([jnp.where(low_b, xb, zb), jnp.where(low_b, zb, xb)], axis=0)
                yd2 = _dot(m2, rhs)
                y_s[rws, sl] = yd2 + yo_g[:, pr * LANES:(pr + 1) * LANES] * jnp.exp(ce2)
                xws.append((xdt2 * jnp.exp(ce2[L - 1:L, :] - ce2)).astype(BF16))
            xw_g = jnp.concatenate(xws, axis=1)
            dec_g = jnp.exp(ce_t[r0 + L - 1:r0 + L, g * gw:(g + 1) * gw])
            ST[s, :, g * gw:(g + 1) * gw] = st_g * dec_g + _dot(Bm.T.astype(BF16), xw_g)

    x = x_ref[...]
    m = mod_ref[...]
    h = _rms(x, ng_ref[0:1, :]) * (1.0 + m[:, 1:2, :]) + m[:, 0:1, :]
    hb = h.reshape(rows, d).astype(BF16)
    pre = _dot(hb, wxbc_ref[...]).reshape(sb, tt, cdim)
    buf[:, 8:8 + tt, :] = pre
    cv = _conv_from_buf(buf, pre, cw_ref, cb_ref, w, tt).reshape(rows, cdim)
    cst_out_ref[...] = buf[:, 8 + tt - (w - 1):8 + tt, :]
    buf[:, 0:8, :] = buf[:, tt:tt + 8, :]
    z = _dot(hb, wz_ref[...])
    dt = jax.nn.softplus(_dot(hb, wdt_ref[...]) + dtb_ref[...])
    a_neg = -jnp.exp(alog_ref[...])
    ri = lax.broadcasted_iota(jnp.int32, (rows, rows), 0)
    ci = lax.broadcasted_iota(jnp.int32, (rows, rows), 1)
    tri = jnp.where((ri >= ci) & (ri // L == ci // L), 1.0, 0.0).astype(BF16)
    cum = _dot_exact_lhs(tri, dt * a_neg)

    def expand(v):
        hi = v.astype(BF16)
        mid = (v - hi.astype(F32)).astype(BF16)
        return _dot(jnp.concatenate([hi, mid], axis=1), e2_ref[...])

    xbc_s[...] = cv * jax.nn.sigmoid(cv)
    cum_s[...] = cum
    ce_s[...] = expand(cum)
    dte_s[...] = expand(dt)
    for idx in range(sb * nck):
        chunk(idx)

    y = y_s[...] + dsk_ref[...] * xbc_s[:, 0:di]
    y = _rms(y * (z * jax.nn.sigmoid(z)), sng_ref[...])
    yo = _dot(y.astype(BF16), wout_ref[...]).reshape(sb, tt, d)
    o_ref[...] = x + m[:, 2:3, :] * _rms(yo, ng_ref[1:2, :])

    @pl.when(t == nt - 1)
    def _():
        for s in range(sb):
            sst_out_ref[s] = ST[s].T


def _ssd_call(x, mods, ng, cst, sst, p, sb, tt):
    bn, T, d = x.shape
    di = p["wz"].shape[-1]
    cdim = p["wxbc"].shape[-1]
    nh = di // SSD_HEAD_DIM
    kern = functools.partial(_ssd_kernel, sb=sb, tt=tt)
    out, cst_new, sst_new = pl.pallas_call(
        kern,
        grid=(bn // sb, T // tt),
        in_specs=[
            pl.BlockSpec((sb, tt, d), lambda b, t: (b, t, 0)),
            pl.BlockSpec((sb, 6, d), lambda b, t: (b, 0, 0)),
            _const_spec((4, d)),
            pl.BlockSpec((sb, SSD_CONV - 1, cdim), lambda b, t: (b, 0, 0)),
            pl.BlockSpec((sb, di, SSD_STATE), lambda b, t: (b, 0, 0)),
            _const_spec((d, di)), _const_spec((d, cdim)), _const_spec((d, LANES)),
            _const_spec((SSD_CONV, cdim)), _const_spec((1, cdim)),
            _const_spec((1, LANES)), _const_spec((1, LANES)),
            _const_spec((1, di)), _const_spec((1, di)),
            _const_spec((di, d)), _const_spec((2 * LANES, di)),
        ],
        out_specs=[
            pl.BlockSpec((sb, tt, d), lambda b, t: (b, t, 0)),
            pl.BlockSpec((sb, SSD_CONV - 1, cdim), lambda b, t: (b, 0, 0)),
            pl.BlockSpec((sb, di, SSD_STATE), lambda b, t: (b, 0, 0)),
        ],
        out_shape=[
            jax.ShapeDtypeStruct((bn, T, d), F32),
            jax.ShapeDtypeStruct((bn, SSD_CONV - 1, cdim), F32),
            jax.ShapeDtypeStruct((bn, di, SSD_STATE), F32),
        ],
        scratch_shapes=[
            pltpu.VMEM((sb, tt + 8, cdim), F32),
            pltpu.VMEM((sb * tt, cdim), F32),
            pltpu.VMEM((sb * tt, LANES), F32),
            pltpu.VMEM((sb * tt, di), F32),
            pltpu.VMEM((sb * tt, di), F32),
            pltpu.VMEM((sb * tt, di), F32),
            pltpu.VMEM((sb, SSD_STATE, di), F32),
        ],
        compiler_params=_params(("arbitrary", "arbitrary")),
        name="ssd",
    )(x, mods, ng, cst, sst.reshape(bn, di, SSD_STATE), p["wz"], p["wxbc"], p["wdt"], p["cw"], p["cb"],
      p["dtb"], p["alog"], p["dsk"], p["sng"], p["wout"], p["e2"])
    return out, cst_new, sst_new.reshape(bn, nh, SSD_HEAD_DIM, SSD_STATE)


def _forget_lanes(f):
    nh = FOX_HEADS
    lane = lax.broadcasted_iota(jnp.int32, f.shape, 1)
    hi, mid, lo = (p.astype(F32) for p in _split3(jnp.where(lane < nh, f, 0.0)))
    parts = hi + pltpu.roll(mid, nh, 1) + pltpu.roll(lo, 2 * nh, 1)
    one = jnp.ones_like(f)
    zero = jnp.zeros_like(f)
    qa = jnp.where(lane < 3 * nh, one, jnp.where(lane < 6 * nh, pltpu.roll(parts, 3 * nh, 1), zero))
    ka = jnp.where(lane < 3 * nh, -parts, jnp.where(lane < 6 * nh, one, zero))
    return qa.astype(BF16), ka.astype(BF16)


def _fox_proj_kernel(*refs, sb, tt, past):
    if past:
        (x_ref, mod_ref, ng_ref, lfp_ref, wqkv_ref, wf_ref, bf_ref,
         q_ref, k_ref, v_ref, kb_ref, vb_ref, lf_ref, fq_ref, fp_ref, fcar) = refs
    else:
        (x_ref, mod_ref, ng_ref, wqkv_ref, wf_ref, bf_ref,
         q_ref, k_ref, v_ref, kb_ref, vb_ref, lf_ref, qa_ref, ka_ref, fcar) = refs
    t = pl.program_id(1)
    d = x_ref.shape[-1]
    nh = lf_ref.shape[-1]
    rows = sb * tt
    cblk = min(tt, 256)
    trt = jnp.where(_tril(cblk), 1.0, 0.0).astype(BF16)

    def cumsum_rows(get_rows, n, car):
        outs = []
        for r in range(0, n, cblk):
            c = _dot_exact_lhs(trt, get_rows(r, cblk)) + car
            outs.append(c)
            car = c[cblk - 1:cblk, :]
        return outs, car

    @pl.when(t == 0)
    def _():
        if past:
            plen = lfp_ref.shape[2]
            pb = 256
            triu = jnp.where(lax.broadcasted_iota(jnp.int32, (pb, pb), 0)
                             <= lax.broadcasted_iota(jnp.int32, (pb, pb), 1), 1.0, 0.0).astype(BF16)
            for s in range(sb):
                car = jnp.zeros((nh, 1), F32)
                for r in range(0, plen, pb):
                    hi, mid, lo = _split3(lfp_ref[s, :, r:r + pb])
                    c = _dot(hi, triu) + (_dot(mid, triu) + _dot(lo, triu)) + car
                    fp_ref[s, :, r:r + pb] = c * LOG2E
                    car = c[:, pb - 1:pb]
                eye = (lax.broadcasted_iota(jnp.int32, (nh, LANES), 0)
                       == lax.broadcasted_iota(jnp.int32, (nh, LANES), 1))
                tot = jnp.sum(jnp.where(eye, jnp.broadcast_to(car, (nh, LANES)), 0.0), axis=0, keepdims=True)
                fcar[s] = jnp.broadcast_to(tot, (8, LANES))
        else:
            fcar[...] = jnp.zeros((sb, 8, LANES), F32)

    x = x_ref[...]
    m = mod_ref[...]
    h = _rms(x, ng_ref[0:1, :]) * (1.0 + m[:, 1:2, :]) + m[:, 0:1, :]
    hb = h.reshape(rows, d).astype(BF16)
    qkv = _dot(hb, wqkv_ref[...])
    q = qkv[:, 0:d] * (FOX_HEAD_DIM ** -0.5 * LOG2E)
    k = qkv[:, d:2 * d]
    v = qkv[:, 2 * d:3 * d]
    q_ref[...] = q.astype(BF16).reshape(sb, tt, d)
    k_ref[...] = k.reshape(sb, tt, d)
    v_ref[...] = v.reshape(sb, tt, d)
    kb_ref[...] = k.astype(BF16).reshape(sb, tt, d)
    if past:
        vb_ref[...] = v.astype(BF16).reshape(sb, tt, d)
    else:
        for s in range(sb):
            vb_ref[s] = v[s * tt:(s + 1) * tt, :].T.astype(BF16)
    lf = jax.nn.log_sigmoid(_dot(hb, wf_ref[...]) + bf_ref[...])
    lf_ref[...] = lf[:, 0:nh].reshape(sb, tt, nh)
    for s in range(sb):
        outs, car = cumsum_rows(lambda r, n, s=s: lf[s * tt + r:s * tt + r + n, :], tt, fcar[s, 0:1, :])
        for bi, c in enumerate(outs):
            if past:
                fq_ref[s, bi * cblk:(bi + 1) * cblk, :] = c[:, 0:nh] * LOG2E
            else:
                qa, ka = _forget_lanes(c * LOG2E)
                qa_ref[s, bi * cblk:(bi + 1) * cblk, :] = qa
                ka_ref[s, bi * cblk:(bi + 1) * cblk, :] = ka
        fcar[s] = jnp.broadcast_to(car, (8, LANES))


def _fox_proj_call(x, mods, ng, lf_past, wqkv, wf, bf, sb, tt):
    bn, T, d = x.shape
    nh = FOX_HEADS
    past = lf_past is not None
    kern = functools.partial(_fox_proj_kernel, sb=sb, tt=tt, past=past)
    in_specs = [
        pl.BlockSpec((sb, tt, d), lambda b, t: (b, t, 0)),
        pl.BlockSpec((sb, 6, d), lambda b, t: (b, 0, 0)),
        _const_spec((4, d)),
    ]
    args = [x, mods, ng]
    if past:
        plen = lf_past.shape[2]
        in_specs.append(pl.BlockSpec((sb, nh, plen), lambda b, t: (b, 0, 0)))
        args.append(lf_past)
    in_specs += [_const_spec((d, 3 * d)), _const_spec((d, LANES)), _const_spec((1, LANES))]
    args += [wqkv, wf, jnp.pad(bf.reshape(1, nh), ((0, 0), (0, LANES - nh)))]
    tile = pl.BlockSpec((sb, tt, d), lambda b, t: (b, t, 0))
    small = pl.BlockSpec((sb, tt, nh), lambda b, t: (b, t, 0))
    out_specs = [tile, tile, tile, tile, tile, small, small]
    out_shape = [
        jax.ShapeDtypeStruct((bn, T, d), BF16),
        jax.ShapeDtypeStruct((bn, T, d), F32),
        jax.ShapeDtypeStruct((bn, T, d), F32),
        jax.ShapeDtypeStruct((bn, T, d), BF16),
        jax.ShapeDtypeStruct((bn, T, d), BF16),
        jax.ShapeDtypeStruct((bn, T, nh), F32),
        jax.ShapeDtypeStruct((bn, T, nh), F32),
    ]
    if past:
        out_specs.append(pl.BlockSpec((sb, nh, plen), lambda b, t: (b, 0, 0)))
        out_shape.append(jax.ShapeDtypeStruct((bn, nh, plen), F32))
    else:
        out_specs[4] = pl.BlockSpec((sb, d, tt), lambda b, t: (b, 0, t))
        out_shape[4] = jax.ShapeDtypeStruct((bn, d, T), BF16)
        wide = pl.BlockSpec((sb, tt, LANES), lambda b, t: (b, t, 0))
        out_specs[6:] = [wide, wide]
        out_shape[6:] = [jax.ShapeDtypeStruct((bn, T, LANES), BF16)] * 2
    return pl.pallas_call(
        kern,
        grid=(bn // sb, T // tt),
        in_specs=in_specs,
        out_specs=out_specs,
        out_shape=out_shape,
        scratch_shapes=[pltpu.VMEM((sb, 8, LANES), F32)],
        compiler_params=_params(("arbitrary", "arbitrary")),
        name="fox_proj",
    )(*args)


def _fox_attn_cached_kernel(x_ref, mod_ref, ng_ref, q_ref, kt_ref, vt_ref, fkt_ref, kn_ref, vn_ref, fq_ref,
                            fktn_ref, wo_ref, o_ref, acc, m_s, l_s, fq_s, *, tq):
    j = pl.program_id(1)
    nj = pl.num_programs(1)
    d = x_ref.shape[-1]
    npair = d // LANES
    hd = FOX_HEAD_DIM

    @pl.when(j == 0)
    def _():
        acc[...] = jnp.zeros(acc.shape, F32)
        l_s[...] = jnp.zeros(l_s.shape, F32)
        m_s[...] = jnp.full(m_s.shape, -1e30, F32)
        for jp in range(npair):
            fq_s[jp] = jnp.concatenate([jnp.broadcast_to(fq_ref[:, 2 * jp:2 * jp + 1], (tq, LANES)),
                                        jnp.broadcast_to(fq_ref[:, 2 * jp + 1:2 * jp + 2], (tq, LANES))], axis=0)

    low = lax.broadcasted_iota(jnp.int32, (tq, LANES), 1) < hd

    def attend(jp, s2, fk_a, fk_b, vmat, v_is_transposed, mask=None):
        nk = s2.shape[1]
        t = s2 - jnp.concatenate([jnp.broadcast_to(fk_a, (tq, nk)), jnp.broadcast_to(fk_b, (tq, nk))], axis=0)
        if mask is not None:
            t = jnp.where(mask, t, -jnp.inf)
        fq2 = fq_s[jp]
        m_old = m_s[jp]
        m_new = jnp.maximum(m_old, jnp.max(t, axis=-1, keepdims=True) + fq2)
        alpha = jnp.exp2(m_old - m_new)
        p = jnp.exp2(t - (m_new - fq2)[:, 0:1])
        l_s[jp] = alpha * l_s[jp] + jnp.sum(p, axis=-1, keepdims=True)
        m_s[jp] = m_new
        pb = p.astype(BF16)
        pv = _dot_nt(pb, vmat) if v_is_transposed else _dot(pb, vmat)
        acc[jp] = alpha * acc[jp] + pv

    def q_stack(jp):
        q2 = q_ref[:, jp * LANES:(jp + 1) * LANES]
        zq = jnp.zeros_like(q2)
        return jnp.concatenate([jnp.where(low, q2, zq), jnp.where(low, zq, q2)], axis=0)

    for jp in range(npair):
        kt2 = kt_ref[jp * LANES:(jp + 1) * LANES, :].astype(BF16)
        vt2 = vt_ref[jp * LANES:(jp + 1) * LANES, :].astype(BF16)
        attend(jp, _dot(q_stack(jp), kt2), fkt_ref[2 * jp:2 * jp + 1, :], fkt_ref[2 * jp + 1:2 * jp + 2, :],
               vt2, True)

    @pl.when(j == nj - 1)
    def _():
        rowq = lax.broadcasted_iota(jnp.int32, (2 * tq, LANES), 0)
        rowq = jnp.where(rowq >= tq, rowq - tq, rowq)
        causal = lax.broadcasted_iota(jnp.int32, (2 * tq, LANES), 1) <= rowq
        zpad = jnp.zeros((LANES - tq, LANES), BF16)
        for jp in range(npair):
            kn2 = jnp.concatenate([kn_ref[:, jp * LANES:(jp + 1) * LANES], zpad], axis=0)
            vn2 = jnp.concatenate([vn_ref[:, jp * LANES:(jp + 1) * LANES], zpad], axis=0)
            attend(jp, _dot_nt(q_stack(jp), kn2), fktn_ref[2 * jp:2 * jp + 1, :],
                   fktn_ref[2 * jp + 1:2 * jp + 2, :], vn2, False, causal)
        cols = []
        for jp in range(npair):
            a, l = acc[jp], l_s[jp]
            cols.append((jnp.where(low, a[0:tq], a[tq:2 * tq]) / jnp.where(low, l[0:tq], l[tq:2 * tq])).astype(BF16))
        y = _dot(jnp.concatenate(cols, axis=1), wo_ref[...])
        m = mod_ref[...]
        o_ref[...] = x_ref[...] + m[2:3, :] * _rms(y, ng_ref[1:2, :])


def _fox_attn_cached_call(x, mods, ng, q, kt_past, vt_past, fkt_past, kb_new, vb_new, fq, fkt_new, wo):
    bn, tq, d = x.shape
    plen = kt_past.shape[-1]
    nh = FOX_HEADS
    assert tq <= LANES and plen % 1024 == 0
    tkb = 1024
    kern = functools.partial(_fox_attn_cached_kernel, tq=tq)
    row = pl.BlockSpec((None, tq, d), lambda b, j: (b, 0, 0))
    return pl.pallas_call(
        kern,
        grid=(bn, plen // tkb),
        in_specs=[
            row,
            pl.BlockSpec((None, 6, d), lambda b, j: (b, 0, 0)),
            _const_spec((4, d)),
            row,
            pl.BlockSpec((None, d, tkb), lambda b, j: (b, 0, j)),
            pl.BlockSpec((None, d, tkb), lambda b, j: (b, 0, j)),
            pl.BlockSpec((None, nh, tkb), lambda b, j: (b, 0, j)),
            row,
            row,
            pl.BlockSpec((None, tq, nh), lambda b, j: (b, 0, 0)),
            pl.BlockSpec((None, nh, LANES), lambda b, j: (b, 0, 0)),
            _const_spec((d, d)),
        ],
        out_specs=row,
        out_shape=jax.ShapeDtypeStruct((bn, tq, d), F32),
        scratch_shapes=[
            pltpu.VMEM((d // LANES, 2 * tq, LANES), F32),
            pltpu.VMEM((d // LANES, 2 * tq, LANES), F32),
            pltpu.VMEM((d // LANES, 2 * tq, LANES), F32),
            pltpu.VMEM((d // LANES, 2 * tq, LANES), F32),
        ],
        compiler_params=_params(("arbitrary", "arbitrary")),
        name="fox_attn_cached",
    )(x, mods, ng, q, kt_past, vt_past, fkt_past, kb_new, vb_new, fq, fkt_new, wo)


def _fox_attn_t_kernel(x_ref, mod_ref, ng_ref, q_ref, k_ref, vt_ref, qa_ref, ka_ref, wo_ref,
                       o_ref, acct, m_s, *, tq, tk, rs):
    qi = pl.program_id(1)
    ki = pl.program_id(2)
    nk = pl.num_programs(2)
    d = x_ref.shape[-1]
    npair = d // LANES
    hd = FOX_HEAD_DIM
    nstrip = tk // rs
    nqb = tq // LANES

    @pl.when(ki == 0)
    def _():
        acct[...] = jnp.zeros(acct.shape, F32)
        m_s[...] = jnp.full(m_s.shape, -1e30, F32)

    lane_q = lax.broadcasted_iota(jnp.int32, (tq, LANES), 1)
    low = lane_q < hd
    ones_v = jnp.ones((hd, tk), BF16)

    def step(masked):
        if masked:
            dk = (lax.broadcasted_iota(jnp.int32, (rs, LANES), 0)
                  - lax.broadcasted_iota(jnp.int32, (rs, LANES), 1))
        qa = qa_ref[...]
        ka = ka_ref[...]

        def block_kind(i, c):
            if not masked:
                return 2
            if i * rs > c * LANES + LANES - 1:
                return 0
            if i * rs + rs - 1 <= c * LANES:
                return 2
            return 1

        def logits(st, i, c):
            t = st[i * rs:(i + 1) * rs, c * LANES:(c + 1) * LANES]
            if block_kind(i, c) == 1:
                t = jnp.where(dk <= (c * LANES - i * rs), t, -jnp.inf)
            return t

        def qk(hh):
            j, half = hh // 2, hh % 2
            q2 = q_ref[:, j * LANES:(j + 1) * LANES]
            qh = jnp.where(low if half == 0 else ~low, q2, jnp.zeros_like(q2))
            sel = ((lane_q & (FOX_HEADS - 1)) == hh) & (lane_q < 6 * FOX_HEADS)
            qcat = jnp.concatenate([qh, jnp.where(sel, qa, jnp.zeros_like(qa))], axis=1)
            kcat = jnp.concatenate([k_ref[:, j * LANES:(j + 1) * LANES], ka], axis=1)
            return _dot_nt(kcat, qcat)

        ahead = 2
        sts = [qk(h0) for h0 in range(ahead)]
        for hh in range(2 * npair):
            j, half = hh // 2, hh % 2
            st = sts.pop(0)
            if hh + ahead < 2 * npair:
                sts.append(qk(hh + ahead))
            vt2 = vt_ref[j * LANES:(j + 1) * LANES, :]
            m_old = m_s[hh:hh + 1, :]
            pm = [None] * nqb
            for i in range(nstrip):
                for c in range(nqb):
                    if block_kind(i, c) > 0:
                        t = logits(st, i, c)
                        pm[c] = t if pm[c] is None else jnp.maximum(pm[c], t)
            mx = jnp.concatenate([jnp.max(p_, axis=0, keepdims=True) for p_ in pm], axis=1)
            m_new = jnp.maximum(m_old, mx)
            alpha = jnp.exp2(m_old - m_new)
            prow = []
            for i in range(nstrip):
                pieces = []
                for c in range(nqb):
                    if block_kind(i, c) > 0:
                        pieces.append(jnp.exp2(logits(st, i, c)
                                               - m_new[:, c * LANES:(c + 1) * LANES]).astype(BF16))
                    else:
                        pieces.append(jnp.zeros((rs, LANES), BF16))
                prow.append(jnp.concatenate(pieces, axis=1))
            vaug = jnp.concatenate([vt2[half * hd:(half + 1) * hd, :], ones_v], axis=0)
            ot = _dot(vaug, jnp.concatenate(prow, axis=0))
            acct[hh] = alpha * acct[hh] + ot
            m_s[hh:hh + 1, :] = m_new

    @pl.when(ki == qi)
    def _():
        step(True)

    @pl.when(ki < qi)
    def _():
        step(False)

    @pl.when(ki == nk - 1)
    def _():
        ots = []
        for hh in range(2 * npair):
            a = acct[hh]
            ots.append(a[0:hd] / a[hd:hd + 1])
        o = jnp.concatenate(ots, axis=0).T.astype(BF16)
        y = _dot(o, wo_ref[...])
        m = mod_ref[...]
        o_ref[...] = x_ref[...] + m[2:3, :] * _rms(y, ng_ref[1:2, :])


def _fox_attn_t_call(x, mods, ng, q, kb, vt, qa, ka, wo, tq):
    bn, T, d = x.shape
    nh = FOX_HEADS
    tk = tq
    kern = functools.partial(_fox_attn_t_kernel, tq=tq, tk=tk, rs=32)
    return pl.pallas_call(
        kern,
        grid=(bn, T // tq, T // tk),
        in_specs=[
            pl.BlockSpec((None, tq, d), lambda b, i, j: (b, i, 0)),
            pl.BlockSpec((None, 6, d), lambda b, i, j: (b, 0, 0)),
            _const_spec((4, d)),
            pl.BlockSpec((None, tq, d), lambda b, i, j: (b, i, 0)),
            pl.BlockSpec((None, tk, d), lambda b, i, j: (b, jnp.minimum(j, i), 0)),
            pl.BlockSpec((None, d, tk), lambda b, i, j: (b, 0, jnp.minimum(j, i))),
            pl.BlockSpec((None, tq, LANES), lambda b, i, j: (b, i, 0)),
            pl.BlockSpec((None, tk, LANES), lambda b, i, j: (b, jnp.minimum(j, i), 0)),
            _const_spec((d, d)),
        ],
        out_specs=pl.BlockSpec((None, tq, d), lambda b, i, j: (b, i, 0)),
        out_shape=jax.ShapeDtypeStruct((bn, T, d), F32),
        scratch_shapes=[
            pltpu.VMEM((nh, LANES, tq), F32),
            pltpu.VMEM((nh, tq), F32),
        ],
        compiler_params=_params(("arbitrary", "arbitrary", "arbitrary")),
        name="fox_attn_t",
    )(x, mods, ng, q, kb, vt, qa, ka, wo)


def _prep_weights(W):
    bf = lambda a: a.astype(BF16)
    di = W["ssd_w_out"].shape[0]
    nh = di // SSD_HEAD_DIM
    cdim = W["ssd_conv_w"].shape[-1]
    d = W["ssd_w_in"].shape[0]
    lru = []
    for j in range(W["lru_w_x"].shape[0]):
        lru.append(dict(
            wx=bf(W["lru_w_x"][j]), bx=W["lru_b_x"][j], wy=bf(W["lru_w_y"][j]), by=W["lru_b_y"][j],
            cw=W["lru_conv_w"][j], cb=W["lru_conv_b"][j],
            wai=bf(0.5 * jnp.concatenate([W["lru_w_a"][j], W["lru_w_i"][j]], axis=-1)),
            ba=0.5 * W["lru_b_a"][j], bi=0.5 * W["lru_b_i"][j], lam=W["lru_lambda"][j],
            wo=bf(W["lru_w_o"][j]), bo=W["lru_b_o"][j]))
    w_in = W["ssd_w_in"]
    pad_l = lambda v: jnp.pad(v.reshape(1, -1), ((0, 0), (0, LANES - v.shape[-1])))
    onehot = (jnp.arange(LANES)[:, None] == jnp.arange(di)[None, :] // SSD_HEAD_DIM).astype(BF16)
    ssd = dict(
        e2=jnp.concatenate([onehot, onehot], axis=0),
        wz=bf(w_in[:, :di]), wxbc=bf(w_in[:, di:di + cdim]),
        wdt=bf(jnp.pad(w_in[:, di + cdim:], ((0, 0), (0, LANES - nh)))),
        cw=W["ssd_conv_w"], cb=W["ssd_conv_b"].reshape(1, cdim),
        dtb=pad_l(W["ssd_dt_bias"]), alog=pad_l(W["ssd_a_log"]),
        dsk=jnp.repeat(W["ssd_d"], SSD_HEAD_DIM).reshape(1, di), sng=W["ssd_norm_g"].reshape(1, di),
        wout=bf(W["ssd_w_out"]))
    fox = dict(
        wqkv=bf(W["fox_w_qkv"]),
        wf=bf(jnp.pad(W["fox_w_f"], ((0, 0), (0, LANES - W["fox_w_f"].shape[-1])))),
        bf=W["fox_b_f"], wo=bf(W["fox_w_o"]))
    ffn = dict(wg=bf(W["ffn_w_gate"]), wu=bf(W["ffn_w_up"]), wd=bf(W["ffn_w_down"]),
               cw=W["ffn_conv_w"], cb=W["ffn_conv_b"])
    return dict(lru=lru, ssd=ssd, fox=fox, ffn=ffn, norm_g=W["norm_g"])


def _run_trunk(x, mods, prev, P, cfg):
    bn, T, d = x.shape
    sb, tt, tt_ffn, tq, tk = cfg["sb"], cfg["tt"], cfg["tt_ffn"], cfg["tq"], cfg["tk"]
    depth = mods.shape[0]
    new = {}
    ffn_bufs = []
    for layer in range(depth):
        ng = P["norm_g"][layer]
        ml = mods[layer]
        kind = layer % 3
        tag = "l%d" % layer
        if kind == 0:
            x, cn, hn = _lru_call(x, ml, ng, prev[tag + "_conv"], prev[tag + "_h"], P["lru"][layer // 3], sb, tt)
            new[tag + "_conv"], new[tag + "_h"] = cn, hn
        elif kind == 1:
            x, cn, sn = _ssd_call(x, ml, ng, prev[tag + "_conv"], prev[tag + "_ssm"], P["ssd"], cfg["sb_ssd"], tt)
            new[tag + "_conv"], new[tag + "_ssm"] = cn, sn
        else:
            fx = P["fox"]
            kp, vp, lp = prev.get(tag + "_k"), prev.get(tag + "_v"), prev.get(tag + "_logf")
            lpt = None if lp is None else jnp.swapaxes(lp, 1, 2)
            res = _fox_proj_call(x, ml, ng, lpt, fx["wqkv"], fx["wf"], fx["bf"], sb, tt)
            q, k, v, kb, vb, lf, fq = res[:7]
            if kp is None:
                x = _fox_attn_t_call(x, ml, ng, q, kb, vb, res[6], res[7], fx["wo"], tq)
            else:
                plen = kp.shape[1]
                kt_past = jnp.transpose(kp, (0, 2, 3, 1)).reshape(bn, d, plen)
                vt_past = jnp.transpose(vp, (0, 2, 3, 1)).reshape(bn, d, plen)
                fkt_new = jnp.pad(jnp.swapaxes(fq, 1, 2), ((0, 0), (0, 0), (0, LANES - T)))
                x = _fox_attn_cached_call(x, ml, ng, q, kt_past, vt_past, res[7], kb, vb, fq, fkt_new, fx["wo"])
            new[tag + "_k"] = k.reshape(bn, T, FOX_HEADS, FOX_HEAD_DIM)
            new[tag + "_v"] = v.reshape(bn, T, FOX_HEADS, FOX_HEAD_DIM)
            new[tag + "_logf"] = lf
        f = P["ffn"]
        x, buf = _ffn_call(x, ml, ng, prev["ffn_conv"][layer], f["wg"], f["wu"], f["cw"][layer], f["cb"][layer],
                           f["wd"], layer, sb, tt_ffn)
        ffn_bufs.append(buf)
    new["ffn_conv"] = jnp.stack(ffn_bufs)
    return x, new


def _fresh_state(bn, W):
    dr = W["lru_w_x"].shape[-1]
    cdim = W["ssd_conv_w"].shape[-1]
    di = W["ssd_w_out"].shape[0]
    f = W["ffn_w_gate"].shape[-1]
    depth = W["ffn_w_gate"].shape[0]
    st = dict(ffn_conv=jnp.zeros((depth, bn, FFN_CONV - 1, f), F32))
    for layer in range(depth):
        tag = "l%d" % layer
        if layer % 3 == 0:
            st[tag + "_conv"] = jnp.zeros((bn, LRU_CONV - 1, dr), F32)
            st[tag + "_h"] = jnp.zeros((bn, dr), F32)
        elif layer % 3 == 1:
            st[tag + "_conv"] = jnp.zeros((bn, SSD_CONV - 1, cdim), F32)
            st[tag + "_ssm"] = jnp.zeros((bn, di // SSD_HEAD_DIM, SSD_HEAD_DIM, SSD_STATE), F32)
    return st


def _tile_cfg(bn, T):
    if T >= 256:
        return dict(sb=1, sb_ssd=1, tt=256, tt_ffn=256, tq=512 if T % 512 == 0 else 256,
                    tk=512 if T % 512 == 0 else 256)
    sb = 4 if bn % 4 == 0 else 1
    return dict(sb=sb, sb_ssd=2 if bn % 2 == 0 else 1, tt=T, tt_ffn=T, tq=T, tk=128)


def kernel(x_prompt, x_sample, c_prompt, c_sample, state_l0_conv, state_l0_h, state_l1_conv, state_l1_ssm, cache_l2_k, cache_l2_v, cache_l2_logf, state_l3_conv, state_l3_h, state_ffn_conv, ada_w, ada_b, norm_g, lru_w_x, lru_b_x, lru_w_y, lru_b_y, lru_conv_w, lru_conv_b, lru_w_a, lru_b_a, lru_w_i, lru_b_i, lru_lambda, lru_w_o, lru_b_o, ssd_w_in, ssd_conv_w, ssd_conv_b, ssd_dt_bias, ssd_a_log, ssd_d, ssd_norm_g, ssd_w_out, fox_w_qkv, fox_w_f, fox_b_f, fox_w_o, ffn_w_gate, ffn_w_up, ffn_conv_w, ffn_conv_b, ffn_w_down):
    W = dict(ada_w=ada_w, ada_b=ada_b, norm_g=norm_g,
             lru_w_x=lru_w_x, lru_b_x=lru_b_x, lru_w_y=lru_w_y, lru_b_y=lru_b_y,
             lru_conv_w=lru_conv_w, lru_conv_b=lru_conv_b, lru_w_a=lru_w_a, lru_b_a=lru_b_a,
             lru_w_i=lru_w_i, lru_b_i=lru_b_i, lru_lambda=lru_lambda, lru_w_o=lru_w_o, lru_b_o=lru_b_o,
             ssd_w_in=ssd_w_in, ssd_conv_w=ssd_conv_w, ssd_conv_b=ssd_conv_b, ssd_dt_bias=ssd_dt_bias,
             ssd_a_log=ssd_a_log, ssd_d=ssd_d, ssd_norm_g=ssd_norm_g, ssd_w_out=ssd_w_out,
             fox_w_qkv=fox_w_qkv, fox_w_f=fox_w_f, fox_b_f=fox_b_f, fox_w_o=fox_w_o,
             ffn_w_gate=ffn_w_gate, ffn_w_up=ffn_w_up, ffn_conv_w=ffn_conv_w, ffn_conv_b=ffn_conv_b,
             ffn_w_down=ffn_w_down)
    P = _prep_weights(W)
    bp, bs = x_prompt.shape[0], x_sample.shape[0]
    d = x_prompt.shape[-1]
    depth = ada_w.shape[0]
    mods = _ada_call(jnp.concatenate([c_prompt, c_sample], axis=0), ada_w, ada_b)
    mods = mods.reshape(depth, bp + bs, 6, d)

    y_prompt, p = _run_trunk(x_prompt, mods[:, :bp], _fresh_state(bp, W), P, _tile_cfg(bp, x_prompt.shape[1]))
    prev = dict(l0_conv=state_l0_conv, l0_h=state_l0_h, l1_conv=state_l1_conv, l1_ssm=state_l1_ssm,
                l2_k=cache_l2_k, l2_v=cache_l2_v, l2_logf=cache_l2_logf,
                l3_conv=state_l3_conv, l3_h=state_l3_h, ffn_conv=state_ffn_conv)
    y_sample, s = _run_trunk(x_sample, mods[:, bp:], prev, P, _tile_cfg(bs, x_sample.shape[1]))
    return (y_prompt, y_sample,
            p['l0_conv'], p['l0_h'], p['l1_conv'], p['l1_ssm'], p['l2_k'], p['l2_v'], p['l2_logf'],
            p['l3_conv'], p['l3_h'], p['ffn_conv'],
            s['l0_conv'], s['l0_h'], s['l1_conv'], s['l1_ssm'], s['l2_k'], s['l2_v'], s['l2_logf'],
            s['l3_conv'], s['l3_h'], s['ffn_conv'])
```

```python
import functools

import jax
import jax.numpy as jnp
from jax import lax
from jax.experimental import pallas as pl
from jax.experimental.pallas import tpu as pltpu

F32 = jnp.float32
BF16 = jnp.bfloat16

EPS = 1e-6
LOG2E = 1.4426950408889634
LRU_C = 8.0
LRU_BLOCKS = 8
LRU_CONV = 4
SSD_HEAD_DIM = 64
SSD_GROUPS = 4
SSD_STATE = 128
SSD_CONV = 4
SSD_CHUNK = 64
FOX_HEADS = 16
FOX_HEAD_DIM = 64
FFN_CONV = 3

LANES = 128
SUBLANES = 8
VMEM_LIMIT = 56 * 1024 * 1024


def _const_spec(shape):
    nd = len(shape)
    return pl.BlockSpec(shape, lambda *_: (0,) * nd, pipeline_mode=pl.Buffered(1))


def _params(sem):
    return pltpu.CompilerParams(dimension_semantics=sem, vmem_limit_bytes=VMEM_LIMIT)


def _rms(x, g):
    return x * lax.rsqrt(jnp.mean(x * x, axis=-1, keepdims=True) + EPS) * g


def _dot(a, b):
    return jnp.dot(a, b, preferred_element_type=F32)


def _dot_nt(a, b):
    return lax.dot_general(a, b, (((1,), (1,)), ((), ())), preferred_element_type=F32)


def _dot_tn(a, b):
    return lax.dot_general(a, b, (((0,), (0,)), ((), ())), preferred_element_type=F32)


def _split3(x):
    hi = x.astype(BF16)
    r1 = x - hi.astype(F32)
    mid = r1.astype(BF16)
    lo = (r1 - mid.astype(F32)).astype(BF16)
    return hi, mid, lo


def _dot_exact_lhs(m, x):
    hi, mid, lo = _split3(x)
    return _dot(m, hi) + (_dot(m, mid) + _dot(m, lo))


def _expm1_nonpos(x):
    u = jnp.exp(x)
    small = (u - 1.0) * x / jnp.log(u)
    return jnp.where(x < -1.0, u - 1.0, jnp.where(u == 1.0, x, small))


def _gelu_tanh(x):
    c0 = 0.7978845608028654
    hx = 0.5 * x
    return hx + hx * jnp.tanh(x * (c0 + (c0 * 0.044715) * (x * x)))


def _tril(n):
    r = lax.broadcasted_iota(jnp.int32, (n, n), 0)
    c = lax.broadcasted_iota(jnp.int32, (n, n), 1)
    return r >= c


def _ada_kernel(c_ref, w_ref, b_ref, o_ref):
    c = c_ref[...]
    s = (c * jax.nn.sigmoid(c)).astype(BF16)
    o_ref[...] = _dot(s, w_ref[...].astype(BF16)) + b_ref[...]


def _ada_call(c_all, ada_w, ada_b):
    depth, d, n = ada_w.shape
    bn = c_all.shape[0]
    tn = 1536 if n % 1536 == 0 else n
    return pl.pallas_call(
        _ada_kernel,
        grid=(depth, n // tn),
        in_specs=[
            pl.BlockSpec((bn, d), lambda l, j: (0, 0)),
            pl.BlockSpec((None, d, tn), lambda l, j: (l, 0, j)),
            pl.BlockSpec((None, 1, tn), lambda l, j: (l, 0, j)),
        ],
        out_specs=pl.BlockSpec((None, bn, tn), lambda l, j: (l, 0, j)),
        out_shape=jax.ShapeDtypeStruct((depth, bn, n), F32),
        compiler_params=_params(("arbitrary", "arbitrary")),
        name="ada",
    )(c_all, ada_w, ada_b.reshape(depth, 1, n))


def _conv_from_buf(buf, cur, w_ref, b_ref, width, tt):
    y = b_ref[...] + w_ref[width - 1:width, :] * cur
    for j in range(1, width):
        y = y + w_ref[width - 1 - j:width - j, :] * buf[:, 8 - j:8 - j + tt, :]
    return y


def _ffn_kernel(x_ref, mod_ref, ng_ref, st_ref, wg_ref, wu_ref, cw_ref, cb_ref, wd_ref,
                o_ref, st_out_ref, buf, *, sb, tt):
    t = pl.program_id(1)
    d = x_ref.shape[-1]
    f = wg_ref.shape[-1]
    w = FFN_CONV

    @pl.when(t == 0)
    def _():
        buf[:, 0:8, :] = jnp.zeros((sb, 8, f), F32)
        buf[:, 8 - (w - 1):8, :] = st_ref[...]

    x = x_ref[...]
    m = mod_ref[...]
    h = _rms(x, ng_ref[2:3, :]) * (1.0 + m[:, 4:5, :]) + m[:, 3:4, :]
    hb = h.reshape(sb * tt, d).astype(BF16)
    gp = _dot(hb, wg_ref[...]).reshape(sb, tt, f)
    up = _dot(hb, wu_ref[...])
    buf[:, 8:8 + tt, :] = gp
    g = _conv_from_buf(buf, gp, cw_ref, cb_ref, w, tt)
    st_out_ref[...] = buf[:, 8 + tt - (w - 1):8 + tt, :]
    buf[:, 0:8, :] = buf[:, tt:tt + 8, :]
    act = (_gelu_tanh(g).reshape(sb * tt, f) * up).astype(BF16)
    y = _dot(act, wd_ref[...]).reshape(sb, tt, d)
    o_ref[...] = x + m[:, 5:6, :] * _rms(y, ng_ref[3:4, :])


def _ffn_call(x, mods, ng, st, wg, wu, cw, cb, wd, layer, sb, tt):
    bn, T, d = x.shape
    f = wg.shape[-1]
    kern = functools.partial(_ffn_kernel, sb=sb, tt=tt)

    def layer_spec(shape):
        return pl.BlockSpec((None,) + shape, lambda b, t: (layer, 0, 0), pipeline_mode=pl.Buffered(1))

    return pl.pallas_call(
        kern,
        grid=(bn // sb, T // tt),
        in_specs=[
            pl.BlockSpec((sb, tt, d), lambda b, t: (b, t, 0)),
            pl.BlockSpec((sb, 6, d), lambda b, t: (b, 0, 0)),
            _const_spec((4, d)),
            pl.BlockSpec((sb, FFN_CONV - 1, f), lambda b, t: (b, 0, 0)),
            layer_spec((d, f)),
            layer_spec((d, f)),
            _const_spec((FFN_CONV, f)),
            _const_spec((1, f)),
            layer_spec((f, d)),
        ],
        out_specs=[
            pl.BlockSpec((sb, tt, d), lambda b, t: (b, t, 0)),
            pl.BlockSpec((sb, FFN_CONV - 1, f), lambda b, t: (b, 0, 0)),
        ],
        out_shape=[
            jax.ShapeDtypeStruct((bn, T, d), F32),
            jax.ShapeDtypeStruct((bn, FFN_CONV - 1, f), F32),
        ],
        scratch_shapes=[pltpu.VMEM((sb, tt + 8, f), F32)],
        compiler_params=_params(("arbitrary", "arbitrary")),
        name="ffn",
    )(x, mods, ng, st, wg, wu, cw, cb.reshape(1, f), wd)


def _lru_pitch(tt):
    seg = tt // 8
    assert seg % 8 == 0
    return seg if (seg // 8) % 2 == 1 else seg + 8


def _lru_kernel(x_ref, mod_ref, ng_ref, cst_ref, hst_ref, wx_ref, bx_ref, wy_ref, by_ref,
                cw_ref, cb_ref, wai_ref, ba_ref, bi_ref, lam_ref, wo_ref, bo_ref,
                o_ref, cst_out_ref, hst_out_ref, buf, za_s, zi_s, u_s, g_s, hcar, *, sb, tt):
    t = pl.program_id(1)
    d = x_ref.shape[-1]
    dr = wx_ref.shape[-1]
    bw = dr // LRU_BLOCKS
    w = LRU_CONV
    rows = sb * tt

    @pl.when(t == 0)
    def _():
        buf[:, 0:8, :] = jnp.zeros((sb, 8, dr), F32)
        buf[:, 8 - (w - 1):8, :] = cst_ref[...]
        hcar[...] = jnp.broadcast_to(hst_ref[...], (sb, 8, dr))

    seg = tt // 8
    pitch = _lru_pitch(tt)
    nsl = dr // LANES

    def stage(ref, val, n):
        for s in range(sb):
            for k in range(8):
                r = (s * 8 + k) * pitch
                ref[n, r:r + seg, :] = val[s * tt + k * seg:s * tt + (k + 1) * seg, :]

    khalf = (-0.5 * LRU_C) * jax.nn.softplus(-lam_ref[...])
    row8 = lax.broadcasted_iota(jnp.int32, (8, LANES), 0)

    def scan_slabs(lo, hi):
        for s in range(sb):
            for n in range(lo, hi):
                ls = slice(n * LANES, (n + 1) * LANES)
                kh = khalf[:, ls]

                def rows_of(g, s=s):
                    return pl.ds(s * 8 * pitch + g, 8, stride=pitch)

                hloc = jnp.zeros((8, LANES), F32)
                prod = jnp.ones((8, LANES), F32)
                hlocs, prods = [], []
                for g in range(seg):
                    log_a = kh + kh * jnp.tanh(za_s[n, rows_of(g), :] + ba_ref[:, ls])
                    hu = 0.5 * u_s[n, rows_of(g), :]
                    iu = hu + hu * jnp.tanh(zi_s[n, rows_of(g), :] + bi_ref[:, ls])
                    a = jnp.exp(log_a)
                    hloc = a * hloc + jnp.sqrt(jnp.tanh(-log_a) * (1.0 + a * a)) * iu
                    prod = a * prod
                    hlocs.append(hloc)
                    prods.append(prod)
                h0 = hcar[s, :, ls]
                hin = h0
                for _ in range(7):
                    hin = jnp.where(row8 == 0, h0, pltpu.roll(hloc + prod * hin, 1, 0))
                hcar[s, :, ls] = jnp.broadcast_to((hloc + prod * hin)[7:8, :], (8, LANES))
                for g in range(seg):
                    g_s[n, rows_of(g), :] = (hlocs[g] + prods[g] * hin) * _gelu_tanh(g_s[n, rows_of(g), :])

    x = x_ref[...]
    m = mod_ref[...]
    h = _rms(x, ng_ref[0:1, :]) * (1.0 + m[:, 1:2, :]) + m[:, 0:1, :]
    hb = h.reshape(rows, d).astype(BF16)
    xx = (_dot(hb, wx_ref[...]) + bx_ref[...]).reshape(sb, tt, dr)
    buf[:, 8:8 + tt, :] = xx
    u = _conv_from_buf(buf, xx, cw_ref, cb_ref, w, tt).reshape(rows, dr)
    cst_out_ref[...] = buf[:, 8 + tt - (w - 1):8 + tt, :]
    buf[:, 0:8, :] = buf[:, tt:tt + 8, :]
    gp = _dot(hb, wy_ref[...]) + by_ref[...]
    ub = u.astype(BF16)
    for n in range(LRU_BLOCKS):
        z = _dot(ub[:, n * bw:(n + 1) * bw], wai_ref[n])
        stage(za_s, z[:, :bw], n)
        stage(zi_s, z[:, bw:], n)
        stage(u_s, u[:, n * bw:(n + 1) * bw], n)
        stage(g_s, gp[:, n * bw:(n + 1) * bw], n)
    scan_slabs(0, nsl)
    hst_out_ref[...] = hcar[:, 7:8, :]

    cols = []
    for n in range(nsl):
        cols.append(jnp.concatenate([g_s[n, (s * 8 + k) * pitch:(s * 8 + k) * pitch + seg, :]
                                     for s in range(sb) for k in range(8)], axis=0))
    y = (_dot(jnp.concatenate(cols, axis=1).astype(BF16), wo_ref[...]) + bo_ref[...]).reshape(sb, tt, d)
    o_ref[...] = x + m[:, 2:3, :] * _rms(y, ng_ref[1:2, :])


def _lru_call(x, mods, ng, cst, hst, p, sb, tt):
    bn, T, d = x.shape
    dr = p["wx"].shape[-1]
    bw = dr // LRU_BLOCKS
    kern = functools.partial(_lru_kernel, sb=sb, tt=tt)
    row = lambda v: v.reshape(1, -1)
    slab = pltpu.VMEM((dr // LANES, sb * 8 * _lru_pitch(tt), LANES), F32)
    out, cst_new, hst_new = pl.pallas_call(
        kern,
        grid=(bn // sb, T // tt),
        in_specs=[
            pl.BlockSpec((sb, tt, d), lambda b, t: (b, t, 0)),
            pl.BlockSpec((sb, 6, d), lambda b, t: (b, 0, 0)),
            _const_spec((4, d)),
            pl.BlockSpec((sb, LRU_CONV - 1, dr), lambda b, t: (b, 0, 0)),
            pl.BlockSpec((sb, 1, dr), lambda b, t: (b, 0, 0)),
            _const_spec((d, dr)), _const_spec((1, dr)),
            _const_spec((d, dr)), _const_spec((1, dr)),
            _const_spec((LRU_CONV, dr)), _const_spec((1, dr)),
            _const_spec((LRU_BLOCKS, bw, 2 * bw)), _const_spec((1, dr)), _const_spec((1, dr)),
            _const_spec((1, dr)),
            _const_spec((dr, d)), _const_spec((1, d)),
        ],
        out_specs=[
            pl.BlockSpec((sb, tt, d), lambda b, t: (b, t, 0)),
            pl.BlockSpec((sb, LRU_CONV - 1, dr), lambda b, t: (b, 0, 0)),
            pl.BlockSpec((sb, 1, dr), lambda b, t: (b, 0, 0)),
        ],
        out_shape=[
            jax.ShapeDtypeStruct((bn, T, d), F32),
            jax.ShapeDtypeStruct((bn, LRU_CONV - 1, dr), F32),
            jax.ShapeDtypeStruct((bn, 1, dr), F32),
        ],
        scratch_shapes=[pltpu.VMEM((sb, tt + 8, dr), F32), slab, slab, slab, slab, pltpu.VMEM((sb, 8, dr), F32)],
        compiler_params=_params(("arbitrary", "arbitrary")),
        name="lru",
    )(x, mods, ng, cst, hst.reshape(bn, 1, dr), p["wx"], row(p["bx"]), p["wy"], row(p["by"]),
      p["cw"], row(p["cb"]), p["wai"], row(p["ba"]), row(p["bi"]), row(p["lam"]),
      p["wo"], row(p["bo"]))
    return out, cst_new, hst_new.reshape(bn, dr)


def _ssd_kernel(x_ref, mod_ref, ng_ref, cst_ref, sst_ref, wz_ref, wxbc_ref, wdt_ref, cw_ref, cb_ref,
                dtb_ref, alog_ref, dsk_ref, sng_ref, wout_ref, e2_ref,
                o_ref, cst_out_ref, sst_out_ref, buf, xbc_s, cum_s, ce_s, dte_s, y_s, ST, *, sb, tt):
    t = pl.program_id(1)
    nt = pl.num_programs(1)
    d = x_ref.shape[-1]
    di = wz_ref.shape[-1]
    cdim = wxbc_ref.shape[-1]
    w = SSD_CONV
    L = SSD_CHUNK
    P = SSD_HEAD_DIM
    N = SSD_STATE
    G = SSD_GROUPS
    hpg = di // P // G
    rows = sb * tt

    @pl.when(t == 0)
    def _():
        buf[:, 0:8, :] = jnp.zeros((sb, 8, cdim), F32)
        buf[:, 8 - (w - 1):8, :] = cst_ref[...]
        for s in range(sb):
            ST[s] = sst_ref[s].T

    nck = tt // L
    lane2 = lax.broadcasted_iota(jnp.int32, (L, LANES), 1)
    low_b = lane2 < P
    tril2 = lax.broadcasted_iota(jnp.int32, (L, LANES), 0) >= (lane2 & (P - 1))
    gw = hpg * P

    def chunk(idx):
        s = idx // nck
        r0 = idx * L
        rws = slice(r0, r0 + L)
        xbc_t, ce_t, dte_t = xbc_s, ce_s, dte_s
        cum_t = cum_s[rws, :].T
        for g in range(G):
            Bm = xbc_t[rws, di + g * N:di + (g + 1) * N]
            Cm = xbc_t[rws, di + G * N + g * N:di + G * N + (g + 1) * N]
            Bb = Bm.astype(BF16)
            Cb = Cm.astype(BF16)
            cb2 = _dot_nt(Cb, jnp.concatenate([Bb, Bb], axis=0))
            st_g = ST[s, :, g * gw:(g + 1) * gw]
            yo_g = _dot(Cb, st_g.astype(BF16))
            xws = []
            for pr in range(hpg // 2):
                j = g * (hpg // 2) + pr
                sl = slice(j * LANES, (j + 1) * LANES)
                ce2 = ce_t[rws, sl]
                xdt2 = xbc_t[rws, sl] * dte_t[rws, sl]
                crow2 = jnp.concatenate([cum_t[2 * j:2 * j + 1, :], cum_t[2 * j + 1:2 * j + 2, :]], axis=1)
                m2 = (cb2 * jnp.exp(jnp.where(tril2, ce2 - crow2, -jnp.inf))).astype(BF16)
                xb = xdt2.astype(BF16)
                zb = jnp.zeros_like(xb)
                rhs = jnp.concatenate([jnp.where(low_b, xb, zb), jnp.where(low_b, zb, xb)], axis=0)
                yd2 = _dot(m2, rhs)
                y_s[rws, sl] = yd2 + yo_g[:, pr * LANES:(pr + 1) * LANES] * jnp.exp(ce2)
                xws.append((xdt2 * jnp.exp(ce2[L - 1:L, :] - ce2)).astype(BF16))
            xw_g = jnp.concatenate(xws, axis=1)
            dec_g = jnp.exp(ce_t[r0 + L - 1:r0 + L, g * gw:(g + 1) * gw])
            ST[s, :, g * gw:(g + 1) * gw] = st_g * dec_g + _dot(Bm.T.astype(BF16), xw_g)

    x = x_ref[...]
    m = mod_ref[...]
    h = _rms(x, ng_ref[0:1, :]) * (1.0 + m[:, 1:2, :]) + m[:, 0:1, :]
    hb = h.reshape(rows, d).astype(BF16)
    pre = _dot(hb, wxbc_ref[...]).reshape(sb, tt, cdim)
    buf[:, 8:8 + tt, :] = pre
    cv = _conv_from_buf(buf, pre, cw_ref, cb_ref, w, tt).reshape(rows, cdim)
    cst_out_ref[...] = buf[:, 8 + tt - (w - 1):8 + tt, :]
    buf[:, 0:8, :] = buf[:, tt:tt + 8, :]
    z = _dot(hb, wz_ref[...])
    dt = jax.nn.softplus(_dot(hb, wdt_ref[...]) + dtb_ref[...])
    a_neg = -jnp.exp(alog_ref[...])
    ri = lax.broadcasted_iota(jnp.int32, (rows, rows), 0)
    ci = lax.broadcasted_iota(jnp.int32, (rows, rows), 1)
    tri = jnp.where((ri >= ci) & (ri // L == ci // L), 1.0, 0.0).astype(BF16)
    cum = _dot_exact_lhs(tri, dt * a_neg)

    def expand(v):
        hi = v.astype(BF16)
        mid = (v - hi.astype(F32)).astype(BF16)
        return _dot(jnp.concatenate([hi, mid], axis=1), e2_ref[...])

    xbc_s[...] = cv * jax.nn.sigmoid(cv)
    cum_s[...] = cum
    ce_s[...] = expand(cum)
    dte_s[...] = expand(dt)
    for idx in range(sb * nck):
        chunk(idx)

    y = y_s[...] + dsk_ref[...] * xbc_s[:, 0:di]
    y = _rms(y * (z * jax.nn.sigmoid(z)), sng_ref[...])
    yo = _dot(y.astype(BF16), wout_ref[...]).reshape(sb, tt, d)
    o_ref[...] = x + m[:, 2:3, :] * _rms(yo, ng_ref[1:2, :])

    @pl.when(t == nt - 1)
    def _():
        for s in range(sb):
            sst_out_ref[s] = ST[s].T


def _ssd_call(x, mods, ng, cst, sst, p, sb, tt):
    bn, T, d = x.shape
    di = p["wz"].shape[-1]
    cdim = p["wxbc"].shape[-1]
    nh = di // SSD_HEAD_DIM
    kern = functools.partial(_ssd_kernel, sb=sb, tt=tt)
    out, cst_new, sst_new = pl.pallas_call(
        kern,
        grid=(bn // sb, T // tt),
        in_specs=[
            pl.BlockSpec((sb, tt, d), lambda b, t: (b, t, 0)),
            pl.BlockSpec((sb, 6, d), lambda b, t: (b, 0, 0)),
            _const_spec((4, d)),
            pl.BlockSpec((sb, SSD_CONV - 1, cdim), lambda b, t: (b, 0, 0)),
            pl.BlockSpec((sb, di, SSD_STATE), lambda b, t: (b, 0, 0)),
            _const_spec((d, di)), _const_spec((d, cdim)), _const_spec((d, LANES)),
            _const_spec((SSD_CONV, cdim)), _const_spec((1, cdim)),
            _const_spec((1, LANES)), _const_spec((1, LANES)),
            _const_spec((1, di)), _const_spec((1, di)),
            _const_spec((di, d)), _const_spec((2 * LANES, di)),
        ],
        out_specs=[
            pl.BlockSpec((sb, tt, d), lambda b, t: (b, t, 0)),
            pl.BlockSpec((sb, SSD_CONV - 1, cdim), lambda b, t: (b, 0, 0)),
            pl.BlockSpec((sb, di, SSD_STATE), lambda b, t: (b, 0, 0)),
        ],
        out_shape=[
            jax.ShapeDtypeStruct((bn, T, d), F32),
            jax.ShapeDtypeStruct((bn, SSD_CONV - 1, cdim), F32),
            jax.ShapeDtypeStruct((bn, di, SSD_STATE), F32),
        ],
        scratch_shapes=[
            pltpu.VMEM((sb, tt + 8, cdim), F32),
            pltpu.VMEM((sb * tt, cdim), F32),
            pltpu.VMEM((sb * tt, LANES), F32),
            pltpu.VMEM((sb * tt, di), F32),
            pltpu.VMEM((sb * tt, di), F32),
            pltpu.VMEM((sb * tt, di), F32),
            pltpu.VMEM((sb, SSD_STATE, di), F32),
        ],
        compiler_params=_params(("arbitrary", "arbitrary")),
        name="ssd",
    )(x, mods, ng, cst, sst.reshape(bn, di, SSD_STATE), p["wz"], p["wxbc"], p["wdt"], p["cw"], p["cb"],
      p["dtb"], p["alog"], p["dsk"], p["sng"], p["wout"], p["e2"])
    return out, cst_new, sst_new.reshape(bn, nh, SSD_HEAD_DIM, SSD_STATE)


def _forget_lanes(f):
    nh = FOX_HEADS
    lane = lax.broadcasted_iota(jnp.int32, f.shape, 1)
    hi, mid, lo = (p.astype(F32) for p in _split3(jnp.where(lane < nh, f, 0.0)))
    parts = hi + pltpu.roll(mid, nh, 1) + pltpu.roll(lo, 2 * nh, 1)
    one = jnp.ones_like(f)
    zero = jnp.zeros_like(f)
    qa = jnp.where(lane < 3 * nh, one, jnp.where(lane < 6 * nh, pltpu.roll(parts, 3 * nh, 1), zero))
    ka = jnp.where(lane < 3 * nh, -parts, jnp.where(lane < 6 * nh, one, zero))
    return qa.astype(BF16), ka.astype(BF16)


def _fox_proj_kernel(*refs, sb, tt, past):
    if past:
        (x_ref, mod_ref, ng_ref, lfp_ref, wqkv_ref, wf_ref, bf_ref,
         q_ref, k_ref, v_ref, kb_ref, vb_ref, lf_ref, fq_ref, fp_ref, fcar) = refs
    else:
        (x_ref, mod_ref, ng_ref, wqkv_ref, wf_ref, bf_ref,
         q_ref, k_ref, v_ref, kb_ref, vb_ref, lf_ref, qa_ref, ka_ref, fcar) = refs
    t = pl.program_id(1)
    d = x_ref.shape[-1]
    nh = lf_ref.shape[-1]
    rows = sb * tt
    cblk = min(tt, 256)
    trt = jnp.where(_tril(cblk), 1.0, 0.0).astype(BF16)

    def cumsum_rows(get_rows, n, car):
        outs = []
        for r in range(0, n, cblk):
            c = _dot_exact_lhs(trt, get_rows(r, cblk)) + car
            outs.append(c)
            car = c[cblk - 1:cblk, :]
        return outs, car

    @pl.when(t == 0)
    def _():
        if past:
            plen = lfp_ref.shape[2]
            pb = 256
            triu = jnp.where(lax.broadcasted_iota(jnp.int32, (pb, pb), 0)
                             <= lax.broadcasted_iota(jnp.int32, (pb, pb), 1), 1.0, 0.0).astype(BF16)
            for s in range(sb):
                car = jnp.zeros((nh, 1), F32)
                for r in range(0, plen, pb):
                    hi, mid, lo = _split3(lfp_ref[s, :, r:r + pb])
                    c = _dot(hi, triu) + (_dot(mid, triu) + _dot(lo, triu)) + car
                    fp_ref[s, :, r:r + pb] = c * LOG2E
                    car = c[:, pb - 1:pb]
                eye = (lax.broadcasted_iota(jnp.int32, (nh, LANES), 0)
                       == lax.broadcasted_iota(jnp.int32, (nh, LANES), 1))
                tot = jnp.sum(jnp.where(eye, jnp.broadcast_to(car, (nh, LANES)), 0.0), axis=0, keepdims=True)
                fcar[s] = jnp.broadcast_to(tot, (8, LANES))
        else:
            fcar[...] = jnp.zeros((sb, 8, LANES), F32)

    x = x_ref[...]
    m = mod_ref[...]
    h = _rms(x, ng_ref[0:1, :]) * (1.0 + m[:, 1:2, :]) + m[:, 0:1, :]
    hb = h.reshape(rows, d).astype(BF16)
    qkv = _dot(hb, wqkv_ref[...])
    q = qkv[:, 0:d] * (FOX_HEAD_DIM ** -0.5 * LOG2E)
    k = qkv[:, d:2 * d]
    v = qkv[:, 2 * d:3 * d]
    q_ref[...] = q.astype(BF16).reshape(sb, tt, d)
    k_ref[...] = k.reshape(sb, tt, d)
    v_ref[...] = v.reshape(sb, tt, d)
    kb_ref[...] = k.astype(BF16).reshape(sb, tt, d)
    if past:
        vb_ref[...] = v.astype(BF16).reshape(sb, tt, d)
    else:
        for s in range(sb):
            vb_ref[s] = v[s * tt:(s + 1) * tt, :].T.astype(BF16)
    lf = jax.nn.log_sigmoid(_dot(hb, wf_ref[...]) + bf_ref[...])
    lf_ref[...] = lf[:, 0:nh].reshape(sb, tt, nh)
    for s in range(sb):
        outs, car = cumsum_rows(lambda r, n, s=s: lf[s * tt + r:s * tt + r + n, :], tt, fcar[s, 0:1, :])
        for bi, c in enumerate(outs):
            if past:
                fq_ref[s, bi * cblk:(bi + 1) * cblk, :] = c[:, 0:nh] * LOG2E
            else:
                qa, ka = _forget_lanes(c * LOG2E)
                qa_ref[s, bi * cblk:(bi + 1) * cblk, :] = qa
                ka_ref[s, bi * cblk:(bi + 1) * cblk, :] = ka
        fcar[s] = jnp.broadcast_to(car, (8, LANES))


def _fox_proj_call(x, mods, ng, lf_past, wqkv, wf, bf, sb, tt):
    bn, T, d = x.shape
    nh = FOX_HEADS
    past = lf_past is not None
    kern = functools.partial(_fox_proj_kernel, sb=sb, tt=tt, past=past)
    in_specs = [
        pl.BlockSpec((sb, tt, d), lambda b, t: (b, t, 0)),
        pl.BlockSpec((sb, 6, d), lambda b, t: (b, 0, 0)),
        _const_spec((4, d)),
    ]
    args = [x, mods, ng]
    if past:
        plen = lf_past.shape[2]
        in_specs.append(pl.BlockSpec((sb, nh, plen), lambda b, t: (b, 0, 0)))
        args.append(lf_past)
    in_specs += [_const_spec((d, 3 * d)), _const_spec((d, LANES)), _const_spec((1, LANES))]
    args += [wqkv, wf, jnp.pad(bf.reshape(1, nh), ((0, 0), (0, LANES - nh)))]
    tile = pl.BlockSpec((sb, tt, d), lambda b, t: (b, t, 0))
    small = pl.BlockSpec((sb, tt, nh), lambda b, t: (b, t, 0))
    out_specs = [tile, tile, tile, tile, tile, small, small]
    out_shape = [
        jax.ShapeDtypeStruct((bn, T, d), BF16),
        jax.ShapeDtypeStruct((bn, T, d), F32),
        jax.ShapeDtypeStruct((bn, T, d), F32),
        jax.ShapeDtypeStruct((bn, T, d), BF16),
        jax.ShapeDtypeStruct((bn, T, d), BF16),
        jax.ShapeDtypeStruct((bn, T, nh), F32),
        jax.ShapeDtypeStruct((bn, T, nh), F32),
    ]
    if past:
        out_specs.append(pl.BlockSpec((sb, nh, plen), lambda b, t: (b, 0, 0)))
        out_shape.append(jax.ShapeDtypeStruct((bn, nh, plen), F32))
    else:
        out_specs[4] = pl.BlockSpec((sb, d, tt), lambda b, t: (b, 0, t))
        out_shape[4] = jax.ShapeDtypeStruct((bn, d, T), BF16)
        wide = pl.BlockSpec((sb, tt, LANES), lambda b, t: (b, t, 0))
        out_specs[6:] = [wide, wide]
        out_shape[6:] = [jax.ShapeDtypeStruct((bn, T, LANES), BF16)] * 2
    return pl.pallas_call(
        kern,
        grid=(bn // sb, T // tt),
        in_specs=in_specs,
        out_specs=out_specs,
        out_shape=out_shape,
        scratch_shapes=[pltpu.VMEM((sb, 8, LANES), F32)],
        compiler_params=_params(("arbitrary", "arbitrary")),
        name="fox_proj",
    )(*args)


def _fox_attn_cached_kernel(x_ref, mod_ref, ng_ref, q_ref, kt_ref, vt_ref, fkt_ref, kn_ref, vn_ref, fq_ref,
                            fktn_ref, wo_ref, o_ref, acc, m_s, l_s, fq_s, *, tq):
    j = pl.program_id(1)
    nj = pl.num_programs(1)
    d = x_ref.shape[-1]
    npair = d // LANES
    hd = FOX_HEAD_DIM

    @pl.when(j == 0)
    def _():
        acc[...] = jnp.zeros(acc.shape, F32)
        l_s[...] = jnp.zeros(l_s.shape, F32)
        m_s[...] = jnp.full(m_s.shape, -1e30, F32)
        for jp in range(npair):
            fq_s[jp] = jnp.concatenate([jnp.broadcast_to(fq_ref[:, 2 * jp:2 * jp + 1], (tq, LANES)),
                                        jnp.broadcast_to(fq_ref[:, 2 * jp + 1:2 * jp + 2], (tq, LANES))], axis=0)

    low = lax.broadcasted_iota(jnp.int32, (tq, LANES), 1) < hd

    def attend(jp, s2, fk_a, fk_b, vmat, v_is_transposed, mask=None):
        nk = s2.shape[1]
        t = s2 - jnp.concatenate([jnp.broadcast_to(fk_a, (tq, nk)), jnp.broadcast_to(fk_b, (tq, nk))], axis=0)
        if mask is not None:
            t = jnp.where(mask, t, -jnp.inf)
        fq2 = fq_s[jp]
        m_old = m_s[jp]
        m_new = jnp.maximum(m_old, jnp.max(t, axis=-1, keepdims=True) + fq2)
        alpha = jnp.exp2(m_old - m_new)
        p = jnp.exp2(t - (m_new - fq2)[:, 0:1])
        l_s[jp] = alpha * l_s[jp] + jnp.sum(p, axis=-1, keepdims=True)
        m_s[jp] = m_new
        pb = p.astype(BF16)
        pv = _dot_nt(pb, vmat) if v_is_transposed else _dot(pb, vmat)
        acc[jp] = alpha * acc[jp] + pv

    def q_stack(jp):
        q2 = q_ref[:, jp * LANES:(jp + 1) * LANES]
        zq = jnp.zeros_like(q2)
        return jnp.concatenate([jnp.where(low, q2, zq), jnp.where(low, zq, q2)], axis=0)

    for jp in range(npair):
        kt2 = kt_ref[jp * LANES:(jp + 1) * LANES, :].astype(BF16)
        vt2 = vt_ref[jp * LANES:(jp + 1) * LANES, :].astype(BF16)
        attend(jp, _dot(q_stack(jp), kt2), fkt_ref[2 * jp:2 * jp + 1, :], fkt_ref[2 * jp + 1:2 * jp + 2, :],
               vt2, True)

    @pl.when(j == nj - 1)
    def _():
        rowq = lax.broadcasted_iota(jnp.int32, (2 * tq, LANES), 0)
        rowq = jnp.where(rowq >= tq, rowq - tq, rowq)
        causal = lax.broadcasted_iota(jnp.int32, (2 * tq, LANES), 1) <= rowq
        zpad = jnp.zeros((LANES - tq, LANES), BF16)
        for jp in range(npair):
            kn2 = jnp.concatenate([kn_ref[:, jp * LANES:(jp + 1) * LANES], zpad], axis=0)
            vn2 = jnp.concatenate([vn_ref[:, jp * LANES:(jp + 1) * LANES], zpad], axis=0)
            attend(jp, _dot_nt(q_stack(jp), kn2), fktn_ref[2 * jp:2 * jp + 1, :],
                   fktn_ref[2 * jp + 1:2 * jp + 2, :], vn2, False, causal)
        cols = []
        for jp in range(npair):
            a, l = acc[jp], l_s[jp]
            cols.append((jnp.where(low, a[0:tq], a[tq:2 * tq]) / jnp.where(low, l[0:tq], l[tq:2 * tq])).astype(BF16))
        y = _dot(jnp.concatenate(cols, axis=1), wo_ref[...])
        m = mod_ref[...]
        o_ref[...] = x_ref[...] + m[2:3, :] * _rms(y, ng_ref[1:2, :])


def _fox_attn_cached_call(x, mods, ng, q, kt_past, vt_past, fkt_past, kb_new, vb_new, fq, fkt_new, wo):
    bn, tq, d = x.shape
    plen = kt_past.shape[-1]
    nh = FOX_HEADS
    assert tq <= LANES and plen % 1024 == 0
    tkb = 1024
    kern = functools.partial(_fox_attn_cached_kernel, tq=tq)
    row = pl.BlockSpec((None, tq, d), lambda b, j: (b, 0, 0))
    return pl.pallas_call(
        kern,
        grid=(bn, plen // tkb),
        in_specs=[
            row,
            pl.BlockSpec((None, 6, d), lambda b, j: (b, 0, 0)),
            _const_spec((4, d)),
            row,
            pl.BlockSpec((None, d, tkb), lambda b, j: (b, 0, j)),
            pl.BlockSpec((None, d, tkb), lambda b, j: (b, 0, j)),
            pl.BlockSpec((None, nh, tkb), lambda b, j: (b, 0, j)),
            row,
            row,
            pl.BlockSpec((None, tq, nh), lambda b, j: (b, 0, 0)),
            pl.BlockSpec((None, nh, LANES), lambda b, j: (b, 0, 0)),
            _const_spec((d, d)),
        ],
        out_specs=row,
        out_shape=jax.ShapeDtypeStruct((bn, tq, d), F32),
        scratch_shapes=[
            pltpu.VMEM((d // LANES, 2 * tq, LANES), F32),
            pltpu.VMEM((d // LANES, 2 * tq, LANES), F32),
            pltpu.VMEM((d // LANES, 2 * tq, LANES), F32),
            pltpu.VMEM((d // LANES, 2 * tq, LANES), F32),
        ],
        compiler_params=_params(("arbitrary", "arbitrary")),
        name="fox_attn_cached",
    )(x, mods, ng, q, kt_past, vt_past, fkt_past, kb_new, vb_new, fq, fkt_new, wo)


def _fox_attn_t_kernel(qi_ref, ki_ref, x_ref, mod_ref, ng_ref, q_ref, k_ref, vt_ref, qa_ref, ka_ref, wo_ref,
                       o_ref, acct, m_s, *, tq, tk, rs):
    step_id = pl.program_id(1)
    qi = qi_ref[step_id]
    ki = ki_ref[step_id]
    d = x_ref.shape[-1]
    npair = d // LANES
    hd = FOX_HEAD_DIM
    nstrip = tk // rs
    nqb = tq // LANES

    @pl.when(ki == 0)
    def _():
        acct[...] = jnp.zeros(acct.shape, F32)
        m_s[...] = jnp.full(m_s.shape, -1e30, F32)

    lane_q = lax.broadcasted_iota(jnp.int32, (tq, LANES), 1)
    low = lane_q < hd
    ones_v = jnp.ones((hd, tk), BF16)

    def step(masked):
        if masked:
            dk = (lax.broadcasted_iota(jnp.int32, (rs, LANES), 0)
                  - lax.broadcasted_iota(jnp.int32, (rs, LANES), 1))
        qa = qa_ref[...]
        ka = ka_ref[...]

        def block_kind(i, c):
            if not masked:
                return 2
            if i * rs > c * LANES + LANES - 1:
                return 0
            if i * rs + rs - 1 <= c * LANES:
                return 2
            return 1

        def logits(st, i, c):
            t = st[i * rs:(i + 1) * rs, c * LANES:(c + 1) * LANES]
            if block_kind(i, c) == 1:
                t = jnp.where(dk <= (c * LANES - i * rs), t, -jnp.inf)
            return t

        def qk(hh):
            j, half = hh // 2, hh % 2
            q2 = q_ref[:, j * LANES:(j + 1) * LANES]
            qh = jnp.where(low if half == 0 else ~low, q2, jnp.zeros_like(q2))
            sel = ((lane_q & (FOX_HEADS - 1)) == hh) & (lane_q < 6 * FOX_HEADS)
            qcat = jnp.concatenate([qh, jnp.where(sel, qa, jnp.zeros_like(qa))], axis=1)
            kcat = jnp.concatenate([k_ref[:, j * LANES:(j + 1) * LANES], ka], axis=1)
            return _dot_nt(kcat, qcat)

        ahead = 2
        sts = [qk(h0) for h0 in range(ahead)]
        for hh in range(2 * npair):
            j, half = hh // 2, hh % 2
            st = sts.pop(0)
            if hh + ahead < 2 * npair:
                sts.append(qk(hh + ahead))
            vt2 = vt_ref[j * LANES:(j + 1) * LANES, :]
            m_old = m_s[hh:hh + 1, :]
            pm = [None] * nqb
            for i in range(nstrip):
                for c in range(nqb):
                    if block_kind(i, c) > 0:
                        t = logits(st, i, c)
                        pm[c] = t if pm[c] is None else jnp.maximum(pm[c], t)
            mx = jnp.concatenate([jnp.max(p_, axis=0, keepdims=True) for p_ in pm], axis=1)
            m_new = jnp.maximum(m_old, mx)
            alpha = jnp.exp2(m_old - m_new)
            prow = []
            for i in range(nstrip):
                pieces = []
                for c in range(nqb):
                    if block_kind(i, c) > 0:
                        pieces.append(jnp.exp2(logits(st, i, c)
                                               - m_new[:, c * LANES:(c + 1) * LANES]).astype(BF16))
                    else:
                        pieces.append(jnp.zeros((rs, LANES), BF16))
                prow.append(jnp.concatenate(pieces, axis=1))
            vaug = jnp.concatenate([vt2[half * hd:(half + 1) * hd, :], ones_v], axis=0)
            ot = _dot(vaug, jnp.concatenate(prow, axis=0))
            acct[hh] = alpha * acct[hh] + ot
            m_s[hh:hh + 1, :] = m_new

    @pl.when(ki == qi)
    def _():
        step(True)

    @pl.when(ki < qi)
    def _():
        step(False)

    @pl.when(ki == qi)
    def _():
        ots = []
        for hh in range(2 * npair):
            a = acct[hh]
            ots.append(a[0:hd] / a[hd:hd + 1])
        o = jnp.concatenate(ots, axis=0).T.astype(BF16)
        y = _dot(o, wo_ref[...])
        m = mod_ref[...]
        o_ref[...] = x_ref[...] + m[2:3, :] * _rms(y, ng_ref[1:2, :])


def _fox_attn_t_call(x, mods, ng, q, kb, vt, qa, ka, wo, tq):
    bn, T, d = x.shape
    nh = FOX_HEADS
    tk = tq
    kern = functools.partial(_fox_attn_t_kernel, tq=tq, tk=tk, rs=32)
    pairs = [(i, j) for i in range(T // tq) for j in range(i + 1)]
    qi_tab = jnp.asarray([p_[0] for p_ in pairs], jnp.int32)
    ki_tab = jnp.asarray([p_[1] for p_ in pairs], jnp.int32)
    qmap = lambda b, n, qt, kt: (b, qt[n], 0)
    kmap = lambda b, n, qt, kt: (b, kt[n], 0)
    grid_spec = pltpu.PrefetchScalarGridSpec(
        num_scalar_prefetch=2,
        grid=(bn, len(pairs)),
        in_specs=[
            pl.BlockSpec((None, tq, d), qmap),
            pl.BlockSpec((None, 6, d), lambda b, n, qt, kt: (b, 0, 0)),
            _const_spec((4, d)),
            pl.BlockSpec((None, tq, d), qmap),
            pl.BlockSpec((None, tk, d), kmap),
            pl.BlockSpec((None, d, tk), lambda b, n, qt, kt: (b, 0, kt[n])),
            pl.BlockSpec((None, tq, LANES), qmap),
            pl.BlockSpec((None, tk, LANES), kmap),
            _const_spec((d, d)),
        ],
        out_specs=pl.BlockSpec((None, tq, d), qmap),
        scratch_shapes=[
            pltpu.VMEM((nh, LANES, tq), F32),
            pltpu.VMEM((nh, tq), F32),
        ],
    )
    return pl.pallas_call(
        kern,
        grid_spec=grid_spec,
        out_shape=jax.ShapeDtypeStruct((bn, T, d), F32),
        compiler_params=_params(("arbitrary", "arbitrary")),
        name="fox_attn_t",
    )(qi_tab, ki_tab, x, mods, ng, q, kb, vt, qa, ka, wo)


def _prep_weights(W):
    bf = lambda a: a.astype(BF16)
    di = W["ssd_w_out"].shape[0]
    nh = di // SSD_HEAD_DIM
    cdim = W["ssd_conv_w"].shape[-1]
    d = W["ssd_w_in"].shape[0]
    lru = []
    for j in range(W["lru_w_x"].shape[0]):
        lru.append(dict(
            wx=bf(W["lru_w_x"][j]), bx=W["lru_b_x"][j], wy=bf(W["lru_w_y"][j]), by=W["lru_b_y"][j],
            cw=W["lru_conv_w"][j], cb=W["lru_conv_b"][j],
            wai=bf(0.5 * jnp.concatenate([W["lru_w_a"][j], W["lru_w_i"][j]], axis=-1)),
            ba=0.5 * W["lru_b_a"][j], bi=0.5 * W["lru_b_i"][j], lam=W["lru_lambda"][j],
            wo=bf(W["lru_w_o"][j]), bo=W["lru_b_o"][j]))
    w_in = W["ssd_w_in"]
    pad_l = lambda v: jnp.pad(v.reshape(1, -1), ((0, 0), (0, LANES - v.shape[-1])))
    onehot = (jnp.arange(LANES)[:, None] == jnp.arange(di)[None, :] // SSD_HEAD_DIM).astype(BF16)
    ssd = dict(
        e2=jnp.concatenate([onehot, onehot], axis=0),
        wz=bf(w_in[:, :di]), wxbc=bf(w_in[:, di:di + cdim]),
        wdt=bf(jnp.pad(w_in[:, di + cdim:], ((0, 0), (0, LANES - nh)))),
        cw=W["ssd_conv_w"], cb=W["ssd_conv_b"].reshape(1, cdim),
        dtb=pad_l(W["ssd_dt_bias"]), alog=pad_l(W["ssd_a_log"]),
        dsk=jnp.repeat(W["ssd_d"], SSD_HEAD_DIM).reshape(1, di), sng=W["ssd_norm_g"].reshape(1, di),
        wout=bf(W["ssd_w_out"]))
    fox = dict(
        wqkv=bf(W["fox_w_qkv"]),
        wf=bf(jnp.pad(W["fox_w_f"], ((0, 0), (0, LANES - W["fox_w_f"].shape[-1])))),
        bf=W["fox_b_f"], wo=bf(W["fox_w_o"]))
    ffn = dict(wg=bf(W["ffn_w_gate"]), wu=bf(W["ffn_w_up"]), wd=bf(W["ffn_w_down"]),
               cw=W["ffn_conv_w"], cb=W["ffn_conv_b"])
    return dict(lru=lru, ssd=ssd, fox=fox, ffn=ffn, norm_g=W["norm_g"])


def _run_trunk(x, mods, prev, P, cfg):
    bn, T, d = x.shape
    sb, tt, tt_ffn, tq = cfg["sb"], cfg["tt"], cfg["tt_ffn"], cfg["tq"]
    depth = mods.shape[0]
    new = {}
    ffn_bufs = []
    for layer in range(depth):
        ng = P["norm_g"][layer]
        ml = mods[layer]
        kind = layer % 3
        tag = "l%d" % layer
        if kind == 0:
            x, cn, hn = _lru_call(x, ml, ng, prev[tag + "_conv"], prev[tag + "_h"], P["lru"][layer // 3], sb,
                                  cfg["tt_big"])
            new[tag + "_conv"], new[tag + "_h"] = cn, hn
        elif kind == 1:
            x, cn, sn = _ssd_call(x, ml, ng, prev[tag + "_conv"], prev[tag + "_ssm"], P["ssd"], cfg["sb_ssd"], tt)
            new[tag + "_conv"], new[tag + "_ssm"] = cn, sn
        else:
            fx = P["fox"]
            kp, vp, lp = prev.get(tag + "_k"), prev.get(tag + "_v"), prev.get(tag + "_logf")
            lpt = None if lp is None else jnp.swapaxes(lp, 1, 2)
            res = _fox_proj_call(x, ml, ng, lpt, fx["wqkv"], fx["wf"], fx["bf"], sb, cfg["tt_big"])
            q, k, v, kb, vb, lf, fq = res[:7]
            if kp is None:
                x = _fox_attn_t_call(x, ml, ng, q, kb, vb, res[6], res[7], fx["wo"], tq)
            else:
                plen = kp.shape[1]
                kt_past = jnp.transpose(kp, (0, 2, 3, 1)).reshape(bn, d, plen)
                vt_past = jnp.transpose(vp, (0, 2, 3, 1)).reshape(bn, d, plen)
                fkt_new = jnp.pad(jnp.swapaxes(fq, 1, 2), ((0, 0), (0, 0), (0, LANES - T)))
                x = _fox_attn_cached_call(x, ml, ng, q, kt_past, vt_past, res[7], kb, vb, fq, fkt_new, fx["wo"])
            new[tag + "_k"] = k.reshape(bn, T, FOX_HEADS, FOX_HEAD_DIM)
            new[tag + "_v"] = v.reshape(bn, T, FOX_HEADS, FOX_HEAD_DIM)
            new[tag + "_logf"] = lf
        f = P["ffn"]
        x, buf = _ffn_call(x, ml, ng, prev["ffn_conv"][layer], f["wg"], f["wu"], f["cw"][layer], f["cb"][layer],
                           f["wd"], layer, sb, tt_ffn)
        ffn_bufs.append(buf)
    new["ffn_conv"] = jnp.stack(ffn_bufs)
    return x, new


def _fresh_state(bn, W):
    dr = W["lru_w_x"].shape[-1]
    cdim = W["ssd_conv_w"].shape[-1]
    di = W["ssd_w_out"].shape[0]
    f = W["ffn_w_gate"].shape[-1]
    depth = W["ffn_w_gate"].shape[0]
    st = dict(ffn_conv=jnp.zeros((depth, bn, FFN_CONV - 1, f), F32))
    for layer in range(depth):
        tag = "l%d" % layer
        if layer % 3 == 0:
            st[tag + "_conv"] = jnp.zeros((bn, LRU_CONV - 1, dr), F32)
            st[tag + "_h"] = jnp.zeros((bn, dr), F32)
        elif layer % 3 == 1:
            st[tag + "_conv"] = jnp.zeros((bn, SSD_CONV - 1, cdim), F32)
            st[tag + "_ssm"] = jnp.zeros((bn, di // SSD_HEAD_DIM, SSD_HEAD_DIM, SSD_STATE), F32)
    return st


def _tile_cfg(bn, T):
    if T >= 256:
        big = 512 if T % 512 == 0 else 256
        return dict(sb=1, sb_ssd=1, tt=256, tt_big=big, tt_ffn=256, tq=big)
    sb = 4 if bn % 4 == 0 else 1
    return dict(sb=sb, sb_ssd=2 if bn % 2 == 0 else 1, tt=T, tt_big=T, tt_ffn=T, tq=T)


def kernel(x_prompt, x_sample, c_prompt, c_sample, state_l0_conv, state_l0_h, state_l1_conv, state_l1_ssm, cache_l2_k, cache_l2_v, cache_l2_logf, state_l3_conv, state_l3_h, state_ffn_conv, ada_w, ada_b, norm_g, lru_w_x, lru_b_x, lru_w_y, lru_b_y, lru_conv_w, lru_conv_b, lru_w_a, lru_b_a, lru_w_i, lru_b_i, lru_lambda, lru_w_o, lru_b_o, ssd_w_in, ssd_conv_w, ssd_conv_b, ssd_dt_bias, ssd_a_log, ssd_d, ssd_norm_g, ssd_w_out, fox_w_qkv, fox_w_f, fox_b_f, fox_w_o, ffn_w_gate, ffn_w_up, ffn_conv_w, ffn_conv_b, ffn_w_down):
    W = dict(ada_w=ada_w, ada_b=ada_b, norm_g=norm_g,
             lru_w_x=lru_w_x, lru_b_x=lru_b_x, lru_w_y=lru_w_y, lru_b_y=lru_b_y,
             lru_conv_w=lru_conv_w, lru_conv_b=lru_conv_b, lru_w_a=lru_w_a, lru_b_a=lru_b_a,
             lru_w_i=lru_w_i, lru_b_i=lru_b_i, lru_lambda=lru_lambda, lru_w_o=lru_w_o, lru_b_o=lru_b_o,
             ssd_w_in=ssd_w_in, ssd_conv_w=ssd_conv_w, ssd_conv_b=ssd_conv_b, ssd_dt_bias=ssd_dt_bias,
             ssd_a_log=ssd_a_log, ssd_d=ssd_d, ssd_norm_g=ssd_norm_g, ssd_w_out=ssd_w_out,
             fox_w_qkv=fox_w_qkv, fox_w_f=fox_w_f, fox_b_f=fox_b_f, fox_w_o=fox_w_o,
             ffn_w_gate=ffn_w_gate, ffn_w_up=ffn_w_up, ffn_conv_w=ffn_conv_w, ffn_conv_b=ffn_conv_b,
             ffn_w_down=ffn_w_down)
    P = _prep_weights(W)
    bp, bs = x_prompt.shape[0], x_sample.shape[0]
    d = x_prompt.shape[-1]
    depth = ada_w.shape[0]
    mods = _ada_call(jnp.concatenate([c_prompt, c_sample], axis=0), ada_w, ada_b)
    mods = mods.reshape(depth, bp + bs, 6, d)

    y_prompt, p = _run_trunk(x_prompt, mods[:, :bp], _fresh_state(bp, W), P, _tile_cfg(bp, x_prompt.shape[1]))
    prev = dict(l0_conv=state_l0_conv, l0_h=state_l0_h, l1_conv=state_l1_conv, l1_ssm=state_l1_ssm,
                l2_k=cache_l2_k, l2_v=cache_l2_v, l2_logf=cache_l2_logf,
                l3_conv=state_l3_conv, l3_h=state_l3_h, ffn_conv=state_ffn_conv)
    y_sample, s = _run_trunk(x_sample, mods[:, bp:], prev, P, _tile_cfg(bs, x_sample.shape[1]))
    return (y_prompt, y_sample,
            p['l0_conv'], p['l0_h'], p['l1_conv'], p['l1_ssm'], p['l2_k'], p['l2_v'], p['l2_logf'],
            p['l3_conv'], p['l3_h'], p['ffn_conv'],
            s['l0_conv'], s['l0_h'], s['l1_conv'], s['l1_ssm'], s['l2_k'], s['l2_v'], s['l2_logf'],
            s['l3_conv'], s['l3_h'], s['ffn_conv'])
```

```python
import functools

import jax
import jax.numpy as jnp
from jax import lax
from jax.experimental import pallas as pl
from jax.experimental.pallas import tpu as pltpu

F32 = jnp.float32
BF16 = jnp.bfloat16

EPS = 1e-6
LOG2E = 1.4426950408889634
LRU_C = 8.0
LRU_BLOCKS = 8
LRU_CONV = 4
SSD_HEAD_DIM = 64
SSD_GROUPS = 4
SSD_STATE = 128
SSD_CONV = 4
SSD_CHUNK = 64
FOX_HEADS = 16
FOX_HEAD_DIM = 64
FFN_CONV = 3

LANES = 128
VMEM_LIMIT = 56 * 1024 * 1024


def _const_spec(shape):
    nd = len(shape)
    return pl.BlockSpec(shape, lambda *_: (0,) * nd, pipeline_mode=pl.Buffered(1))


def _params(sem):
    return pltpu.CompilerParams(dimension_semantics=sem, vmem_limit_bytes=VMEM_LIMIT)


def _rms(x, g):
    return x * lax.rsqrt(jnp.mean(x * x, axis=-1, keepdims=True) + EPS) * g


def _dot(a, b):
    return jnp.dot(a, b, preferred_element_type=F32)


def _dot_nt(a, b):
    return lax.dot_general(a, b, (((1,), (1,)), ((), ())), preferred_element_type=F32)


def _split3(x):
    hi = x.astype(BF16)
    r1 = x - hi.astype(F32)
    mid = r1.astype(BF16)
    lo = (r1 - mid.astype(F32)).astype(BF16)
    return hi, mid, lo


def _dot_exact_lhs(m, x):
    hi, mid, lo = _split3(x)
    return _dot(m, hi) + (_dot(m, mid) + _dot(m, lo))


def _gelu_tanh(x):
    c0 = 0.7978845608028654
    hx = 0.5 * x
    return hx + hx * jnp.tanh(x * (c0 + (c0 * 0.044715) * (x * x)))


def _tril(n):
    r = lax.broadcasted_iota(jnp.int32, (n, n), 0)
    c = lax.broadcasted_iota(jnp.int32, (n, n), 1)
    return r >= c


def _ada_kernel(c_ref, w_ref, b_ref, o_ref):
    c = c_ref[...]
    s = (c * jax.nn.sigmoid(c)).astype(BF16)
    o_ref[...] = _dot(s, w_ref[...].astype(BF16)) + b_ref[...]


def _ada_call(c_all, ada_w, ada_b):
    depth, d, n = ada_w.shape
    bn = c_all.shape[0]
    tn = 1536 if n % 1536 == 0 else n
    return pl.pallas_call(
        _ada_kernel,
        grid=(depth, n // tn),
        in_specs=[
            pl.BlockSpec((bn, d), lambda l, j: (0, 0)),
            pl.BlockSpec((None, d, tn), lambda l, j: (l, 0, j)),
            pl.BlockSpec((None, 1, tn), lambda l, j: (l, 0, j)),
        ],
        out_specs=pl.BlockSpec((None, bn, tn), lambda l, j: (l, 0, j)),
        out_shape=jax.ShapeDtypeStruct((depth, bn, n), F32),
        compiler_params=_params(("arbitrary", "arbitrary")),
        name="ada",
    )(c_all, ada_w, ada_b.reshape(depth, 1, n))


def _conv_from_buf(buf, cur, w_ref, b_ref, width, tt):
    y = b_ref[...] + w_ref[width - 1:width, :] * cur
    for j in range(1, width):
        y = y + w_ref[width - 1 - j:width - j, :] * buf[:, 8 - j:8 - j + tt, :]
    return y


def _ffn_kernel(x_ref, mod_ref, ng_ref, st_ref, wg_ref, wu_ref, cw_ref, cb_ref, wd_ref,
                o_ref, st_out_ref, buf, *, sb, tt):
    t = pl.program_id(1)
    d = x_ref.shape[-1]
    f = wg_ref.shape[-1]
    w = FFN_CONV

    @pl.when(t == 0)
    def _():
        buf[:, 0:8, :] = jnp.zeros((sb, 8, f), F32)
        buf[:, 8 - (w - 1):8, :] = st_ref[...]

    x = x_ref[...]
    m = mod_ref[...]
    h = _rms(x, ng_ref[2:3, :]) * (1.0 + m[:, 4:5, :]) + m[:, 3:4, :]
    hb = h.reshape(sb * tt, d).astype(BF16)
    gp = _dot(hb, wg_ref[...]).reshape(sb, tt, f)
    up = _dot(hb, wu_ref[...])
    buf[:, 8:8 + tt, :] = gp
    g = _conv_from_buf(buf, gp, cw_ref, cb_ref, w, tt)
    st_out_ref[...] = buf[:, 8 + tt - (w - 1):8 + tt, :]
    buf[:, 0:8, :] = buf[:, tt:tt + 8, :]
    act = (_gelu_tanh(g).reshape(sb * tt, f) * up).astype(BF16)
    y = _dot(act, wd_ref[...]).reshape(sb, tt, d)
    o_ref[...] = x + m[:, 5:6, :] * _rms(y, ng_ref[3:4, :])


def _ffn_call(x, mods, ng, st, wg, wu, cw, cb, wd, layer, sb, tt):
    bn, T, d = x.shape
    f = wg.shape[-1]
    kern = functools.partial(_ffn_kernel, sb=sb, tt=tt)

    def layer_spec(shape):
        return pl.BlockSpec((None,) + shape, lambda b, t: (layer, 0, 0), pipeline_mode=pl.Buffered(1))

    return pl.pallas_call(
        kern,
        grid=(bn // sb, T // tt),
        in_specs=[
            pl.BlockSpec((sb, tt, d), lambda b, t: (b, t, 0)),
            pl.BlockSpec((sb, 6, d), lambda b, t: (b, 0, 0)),
            _const_spec((4, d)),
            pl.BlockSpec((sb, FFN_CONV - 1, f), lambda b, t: (b, 0, 0)),
            layer_spec((d, f)),
            layer_spec((d, f)),
            _const_spec((FFN_CONV, f)),
            _const_spec((1, f)),
            layer_spec((f, d)),
        ],
        out_specs=[
            pl.BlockSpec((sb, tt, d), lambda b, t: (b, t, 0)),
            pl.BlockSpec((sb, FFN_CONV - 1, f), lambda b, t: (b, 0, 0)),
        ],
        out_shape=[
            jax.ShapeDtypeStruct((bn, T, d), F32),
            jax.ShapeDtypeStruct((bn, FFN_CONV - 1, f), F32),
        ],
        scratch_shapes=[pltpu.VMEM((sb, tt + 8, f), F32)],
        compiler_params=_params(("arbitrary", "arbitrary")),
        name="ffn",
    )(x, mods, ng, st, wg, wu, cw, cb.reshape(1, f), wd)


def _lru_pitch(tt):
    seg = tt // 8
    assert seg % 8 == 0
    return seg if (seg // 8) % 2 == 1 else seg + 8


def _lru_kernel(x_ref, mod_ref, ng_ref, cst_ref, hst_ref, wx_ref, bx_ref, wy_ref, by_ref,
                cw_ref, cb_ref, wai_ref, ba_ref, bi_ref, lam_ref, wo_ref, bo_ref,
                o_ref, cst_out_ref, hst_out_ref, buf, za_s, zi_s, u_s, g_s, hcar, *, sb, tt):
    t = pl.program_id(1)
    d = x_ref.shape[-1]
    dr = wx_ref.shape[-1]
    bw = dr // LRU_BLOCKS
    w = LRU_CONV
    rows = sb * tt

    @pl.when(t == 0)
    def _():
        buf[:, 0:8, :] = jnp.zeros((sb, 8, dr), F32)
        buf[:, 8 - (w - 1):8, :] = cst_ref[...]
        hcar[...] = jnp.broadcast_to(hst_ref[...], (sb, 8, dr))

    seg = tt // 8
    pitch = _lru_pitch(tt)
    nsl = dr // LANES

    def stage(ref, val, n):
        for s in range(sb):
            for k in range(8):
                r = (s * 8 + k) * pitch
                ref[n, r:r + seg, :] = val[s * tt + k * seg:s * tt + (k + 1) * seg, :]

    khalf = (-0.5 * LRU_C) * jax.nn.softplus(-lam_ref[...])
    row8 = lax.broadcasted_iota(jnp.int32, (8, LANES), 0)

    def scan_slabs(lo, hi):
        for s in range(sb):
            for n in range(lo, hi):
                ls = slice(n * LANES, (n + 1) * LANES)
                kh = khalf[:, ls]

                def rows_of(g, s=s):
                    return pl.ds(s * 8 * pitch + g, 8, stride=pitch)

                hloc = jnp.zeros((8, LANES), F32)
                prod = jnp.ones((8, LANES), F32)
                hlocs, prods = [], []
                for g in range(seg):
                    log_a = kh + kh * jnp.tanh(za_s[n, rows_of(g), :] + ba_ref[:, ls])
                    hu = 0.5 * u_s[n, rows_of(g), :]
                    iu = hu + hu * jnp.tanh(zi_s[n, rows_of(g), :] + bi_ref[:, ls])
                    a = jnp.exp(log_a)
                    hloc = a * hloc + jnp.sqrt(jnp.tanh(-log_a) * (1.0 + a * a)) * iu
                    prod = a * prod
                    hlocs.append(hloc)
                    prods.append(prod)
                h0 = hcar[s, :, ls]
                hin = h0
                for _ in range(7):
                    hin = jnp.where(row8 == 0, h0, pltpu.roll(hloc + prod * hin, 1, 0))
                hcar[s, :, ls] = jnp.broadcast_to((hloc + prod * hin)[7:8, :], (8, LANES))
                for g in range(seg):
                    g_s[n, rows_of(g), :] = (hlocs[g] + prods[g] * hin) * _gelu_tanh(g_s[n, rows_of(g), :])

    x = x_ref[...]
    m = mod_ref[...]
    h = _rms(x, ng_ref[0:1, :]) * (1.0 + m[:, 1:2, :]) + m[:, 0:1, :]
    hb = h.reshape(rows, d).astype(BF16)
    xx = (_dot(hb, wx_ref[...]) + bx_ref[...]).reshape(sb, tt, dr)
    buf[:, 8:8 + tt, :] = xx
    u = _conv_from_buf(buf, xx, cw_ref, cb_ref, w, tt).reshape(rows, dr)
    cst_out_ref[...] = buf[:, 8 + tt - (w - 1):8 + tt, :]
    buf[:, 0:8, :] = buf[:, tt:tt + 8, :]
    gp = _dot(hb, wy_ref[...]) + by_ref[...]
    ub = u.astype(BF16)
    for n in range(LRU_BLOCKS):
        z = _dot(ub[:, n * bw:(n + 1) * bw], wai_ref[n])
        stage(za_s, z[:, :bw], n)
        stage(zi_s, z[:, bw:], n)
        stage(u_s, u[:, n * bw:(n + 1) * bw], n)
        stage(g_s, gp[:, n * bw:(n + 1) * bw], n)
    scan_slabs(0, nsl)
    hst_out_ref[...] = hcar[:, 7:8, :]

    cols = []
    for n in range(nsl):
        cols.append(jnp.concatenate([g_s[n, (s * 8 + k) * pitch:(s * 8 + k) * pitch + seg, :]
                                     for s in range(sb) for k in range(8)], axis=0))
    y = (_dot(jnp.concatenate(cols, axis=1).astype(BF16), wo_ref[...]) + bo_ref[...]).reshape(sb, tt, d)
    o_ref[...] = x + m[:, 2:3, :] * _rms(y, ng_ref[1:2, :])


def _lru_call(x, mods, ng, cst, hst, p, sb, tt):
    bn, T, d = x.shape
    dr = p["wx"].shape[-1]
    bw = dr // LRU_BLOCKS
    kern = functools.partial(_lru_kernel, sb=sb, tt=tt)
    row = lambda v: v.reshape(1, -1)
    slab = pltpu.VMEM((dr // LANES, sb * 8 * _lru_pitch(tt), LANES), F32)
    out, cst_new, hst_new = pl.pallas_call(
        kern,
        grid=(bn // sb, T // tt),
        in_specs=[
            pl.BlockSpec((sb, tt, d), lambda b, t: (b, t, 0)),
            pl.BlockSpec((sb, 6, d), lambda b, t: (b, 0, 0)),
            _const_spec((4, d)),
            pl.BlockSpec((sb, LRU_CONV - 1, dr), lambda b, t: (b, 0, 0)),
            pl.BlockSpec((sb, 1, dr), lambda b, t: (b, 0, 0)),
            _const_spec((d, dr)), _const_spec((1, dr)),
            _const_spec((d, dr)), _const_spec((1, dr)),
            _const_spec((LRU_CONV, dr)), _const_spec((1, dr)),
            _const_spec((LRU_BLOCKS, bw, 2 * bw)), _const_spec((1, dr)), _const_spec((1, dr)),
            _const_spec((1, dr)),
            _const_spec((dr, d)), _const_spec((1, d)),
        ],
        out_specs=[
            pl.BlockSpec((sb, tt, d), lambda b, t: (b, t, 0)),
            pl.BlockSpec((sb, LRU_CONV - 1, dr), lambda b, t: (b, 0, 0)),
            pl.BlockSpec((sb, 1, dr), lambda b, t: (b, 0, 0)),
        ],
        out_shape=[
            jax.ShapeDtypeStruct((bn, T, d), F32),
            jax.ShapeDtypeStruct((bn, LRU_CONV - 1, dr), F32),
            jax.ShapeDtypeStruct((bn, 1, dr), F32),
        ],
        scratch_shapes=[pltpu.VMEM((sb, tt + 8, dr), F32), slab, slab, slab, slab, pltpu.VMEM((sb, 8, dr), F32)],
        compiler_params=_params(("arbitrary", "arbitrary")),
        name="lru",
    )(x, mods, ng, cst, hst.reshape(bn, 1, dr), p["wx"], row(p["bx"]), p["wy"], row(p["by"]),
      p["cw"], row(p["cb"]), p["wai"], row(p["ba"]), row(p["bi"]), row(p["lam"]),
      p["wo"], row(p["bo"]))
    return out, cst_new, hst_new.reshape(bn, dr)


def _ssd_kernel(x_ref, mod_ref, ng_ref, cst_ref, sst_ref, wz_ref, wxbc_ref, wdt_ref, cw_ref, cb_ref,
                dtb_ref, alog_ref, dsk_ref, sng_ref, wout_ref, e2_ref,
                o_ref, cst_out_ref, sst_out_ref, buf, xbc_s, cum_s, ce_s, dte_s, y_s, ST, *, sb, tt):
    t = pl.program_id(1)
    nt = pl.num_programs(1)
    d = x_ref.shape[-1]
    di = wz_ref.shape[-1]
    cdim = wxbc_ref.shape[-1]
    w = SSD_CONV
    L = SSD_CHUNK
    P = SSD_HEAD_DIM
    N = SSD_STATE
    G = SSD_GROUPS
    hpg = di // P // G
    rows = sb * tt

    @pl.when(t == 0)
    def _():
        buf[:, 0:8, :] = jnp.zeros((sb, 8, cdim), F32)
        buf[:, 8 - (w - 1):8, :] = cst_ref[...]
        for s in range(sb):
            ST[s] = sst_ref[s].T

    nck = tt // L
    lane2 = lax.broadcasted_iota(jnp.int32, (L, LANES), 1)
    low_b = lane2 < P
    tril2 = lax.broadcasted_iota(jnp.int32, (L, LANES), 0) >= (lane2 & (P - 1))
    gw = hpg * P

    def chunk(idx):
        s = idx // nck
        r0 = idx * L
        rws = slice(r0, r0 + L)
        xbc_t, ce_t, dte_t = xbc_s, ce_s, dte_s
        cum_t = cum_s[rws, :].T
        bms, cb2s, sts, yos = [], [], [], []
        for g in range(G):
            Bm = xbc_t[rws, di + g * N:di + (g + 1) * N]
            Cb = xbc_t[rws, di + G * N + g * N:di + G * N + (g + 1) * N].astype(BF16)
            Bb = Bm.astype(BF16)
            st_g = ST[s, :, g * gw:(g + 1) * gw]
            bms.append(Bm)
            sts.append(st_g)
            cb2s.append(_dot_nt(Cb, jnp.concatenate([Bb, Bb], axis=0)))
            yos.append(_dot(Cb, st_g.astype(BF16)))
        for g in range(G):
            Bm, cb2, st_g, yo_g = bms[g], cb2s[g], sts[g], yos[g]
            xws = []
            for pr in range(hpg // 2):
                j = g * (hpg // 2) + pr
                sl = slice(j * LANES, (j + 1) * LANES)
                ce2 = ce_t[rws, sl]
                xdt2 = xbc_t[rws, sl] * dte_t[rws, sl]
                crow2 = jnp.concatenate([cum_t[2 * j:2 * j + 1, :], cum_t[2 * j + 1:2 * j + 2, :]], axis=1)
                m2 = (cb2 * jnp.exp(jnp.where(tril2, ce2 - crow2, -jnp.inf))).astype(BF16)
                xb = xdt2.astype(BF16)
                zb = jnp.zeros_like(xb)
                rhs = jnp.concatenate([jnp.where(low_b, xb, zb), jnp.where(low_b, zb, xb)], axis=0)
                yd2 = _dot(m2, rhs)
                y_s[rws, sl] = yd2 + yo_g[:, pr * LANES:(pr + 1) * LANES] * jnp.exp(ce2)
                xws.append((xdt2 * jnp.exp(ce2[L - 1:L, :] - ce2)).astype(BF16))
            xw_g = jnp.concatenate(xws, axis=1)
            dec_g = jnp.exp(ce_t[r0 + L - 1:r0 + L, g * gw:(g + 1) * gw])
            ST[s, :, g * gw:(g + 1) * gw] = st_g * dec_g + _dot(Bm.T.astype(BF16), xw_g)

    x = x_ref[...]
    m = mod_ref[...]
    h = _rms(x, ng_ref[0:1, :]) * (1.0 + m[:, 1:2, :]) + m[:, 0:1, :]
    hb = h.reshape(rows, d).astype(BF16)
    pre = _dot(hb, wxbc_ref[...]).reshape(sb, tt, cdim)
    buf[:, 8:8 + tt, :] = pre
    cv = _conv_from_buf(buf, pre, cw_ref, cb_ref, w, tt).reshape(rows, cdim)
    cst_out_ref[...] = buf[:, 8 + tt - (w - 1):8 + tt, :]
    buf[:, 0:8, :] = buf[:, tt:tt + 8, :]
    z = _dot(hb, wz_ref[...])
    dt = jax.nn.softplus(_dot(hb, wdt_ref[...]) + dtb_ref[...])
    a_neg = -jnp.exp(alog_ref[...])
    ri = lax.broadcasted_iota(jnp.int32, (rows, rows), 0)
    ci = lax.broadcasted_iota(jnp.int32, (rows, rows), 1)
    tri = jnp.where((ri >= ci) & (ri // L == ci // L), 1.0, 0.0).astype(BF16)
    cum = _dot_exact_lhs(tri, dt * a_neg)

    def expand(v):
        hi = v.astype(BF16)
        mid = (v - hi.astype(F32)).astype(BF16)
        return _dot(jnp.concatenate([hi, mid], axis=1), e2_ref[...])

    xbc_s[...] = cv * jax.nn.sigmoid(cv)
    cum_s[...] = cum
    ce_s[...] = expand(cum)
    dte_s[...] = expand(dt)
    for idx in range(sb * nck):
        chunk(idx)

    y = y_s[...] + dsk_ref[...] * xbc_s[:, 0:di]
    y = _rms(y * (z * jax.nn.sigmoid(z)), sng_ref[...])
    yo = _dot(y.astype(BF16), wout_ref[...]).reshape(sb, tt, d)
    o_ref[...] = x + m[:, 2:3, :] * _rms(yo, ng_ref[1:2, :])

    @pl.when(t == nt - 1)
    def _():
        for s in range(sb):
            sst_out_ref[s] = ST[s].T


def _ssd_call(x, mods, ng, cst, sst, p, sb, tt):
    bn, T, d = x.shape
    di = p["wz"].shape[-1]
    cdim = p["wxbc"].shape[-1]
    nh = di // SSD_HEAD_DIM
    kern = functools.partial(_ssd_kernel, sb=sb, tt=tt)
    out, cst_new, sst_new = pl.pallas_call(
        kern,
        grid=(bn // sb, T // tt),
        in_specs=[
            pl.BlockSpec((sb, tt, d), lambda b, t: (b, t, 0)),
            pl.BlockSpec((sb, 6, d), lambda b, t: (b, 0, 0)),
            _const_spec((4, d)),
            pl.BlockSpec((sb, SSD_CONV - 1, cdim), lambda b, t: (b, 0, 0)),
            pl.BlockSpec((sb, di, SSD_STATE), lambda b, t: (b, 0, 0)),
            _const_spec((d, di)), _const_spec((d, cdim)), _const_spec((d, LANES)),
            _const_spec((SSD_CONV, cdim)), _const_spec((1, cdim)),
            _const_spec((1, LANES)), _const_spec((1, LANES)),
            _const_spec((1, di)), _const_spec((1, di)),
            _const_spec((di, d)), _const_spec((2 * LANES, di)),
        ],
        out_specs=[
            pl.BlockSpec((sb, tt, d), lambda b, t: (b, t, 0)),
            pl.BlockSpec((sb, SSD_CONV - 1, cdim), lambda b, t: (b, 0, 0)),
            pl.BlockSpec((sb, di, SSD_STATE), lambda b, t: (b, 0, 0)),
        ],
        out_shape=[
            jax.ShapeDtypeStruct((bn, T, d), F32),
            jax.ShapeDtypeStruct((bn, SSD_CONV - 1, cdim), F32),
            jax.ShapeDtypeStruct((bn, di, SSD_STATE), F32),
        ],
        scratch_shapes=[
            pltpu.VMEM((sb, tt + 8, cdim), F32),
            pltpu.VMEM((sb * tt, cdim), F32),
            pltpu.VMEM((sb * tt, LANES), F32),
            pltpu.VMEM((sb * tt, di), F32),
            pltpu.VMEM((sb * tt, di), F32),
            pltpu.VMEM((sb * tt, di), F32),
            pltpu.VMEM((sb, SSD_STATE, di), F32),
        ],
        compiler_params=_params(("arbitrary", "arbitrary")),
        name="ssd",
    )(x, mods, ng, cst, sst.reshape(bn, di, SSD_STATE), p["wz"], p["wxbc"], p["wdt"], p["cw"], p["cb"],
      p["dtb"], p["alog"], p["dsk"], p["sng"], p["wout"], p["e2"])
    return out, cst_new, sst_new.reshape(bn, nh, SSD_HEAD_DIM, SSD_STATE)


def _forget_lanes(f):
    nh = FOX_HEADS
    lane = lax.broadcasted_iota(jnp.int32, f.shape, 1)
    hi, mid, lo = (p.astype(F32) for p in _split3(jnp.where(lane < nh, f, 0.0)))
    parts = hi + pltpu.roll(mid, nh, 1) + pltpu.roll(lo, 2 * nh, 1)
    one = jnp.ones_like(f)
    zero = jnp.zeros_like(f)
    qa = jnp.where(lane < 3 * nh, one, jnp.where(lane < 6 * nh, pltpu.roll(parts, 3 * nh, 1), zero))
    ka = jnp.where(lane < 3 * nh, -parts, jnp.where(lane < 6 * nh, one, zero))
    return qa.astype(BF16), ka.astype(BF16)


def _fox_proj_kernel(*refs, sb, tt, past):
    if past:
        (x_ref, mod_ref, ng_ref, lfp_ref, wqkv_ref, wf_ref, bf_ref,
         q_ref, k_ref, v_ref, kb_ref, vb_ref, lf_ref, fq_ref, fp_ref, fcar) = refs
    else:
        (x_ref, mod_ref, ng_ref, wqkv_ref, wf_ref, bf_ref,
         q_ref, k_ref, v_ref, kb_ref, vb_ref, lf_ref, qa_ref, ka_ref, fcar) = refs
    t = pl.program_id(1)
    d = x_ref.shape[-1]
    nh = lf_ref.shape[-1]
    rows = sb * tt
    cblk = min(tt, 256)
    trt = jnp.where(_tril(cblk), 1.0, 0.0).astype(BF16)

    def cumsum_rows(get_rows, n, car):
        outs = []
        for r in range(0, n, cblk):
            c = _dot_exact_lhs(trt, get_rows(r, cblk)) + car
            outs.append(c)
            car = c[cblk - 1:cblk, :]
        return outs, car

    @pl.when(t == 0)
    def _():
        if past:
            plen = lfp_ref.shape[2]
            pb = 256
            triu = jnp.where(lax.broadcasted_iota(jnp.int32, (pb, pb), 0)
                             <= lax.broadcasted_iota(jnp.int32, (pb, pb), 1), 1.0, 0.0).astype(BF16)
            for s in range(sb):
                car = jnp.zeros((nh, 1), F32)
                for r in range(0, plen, pb):
                    hi, mid, lo = _split3(lfp_ref[s, :, r:r + pb])
                    c = _dot(hi, triu) + (_dot(mid, triu) + _dot(lo, triu)) + car
                    fp_ref[s, :, r:r + pb] = c * LOG2E
                    car = c[:, pb - 1:pb]
                eye = (lax.broadcasted_iota(jnp.int32, (nh, LANES), 0)
                       == lax.broadcasted_iota(jnp.int32, (nh, LANES), 1))
                tot = jnp.sum(jnp.where(eye, jnp.broadcast_to(car, (nh, LANES)), 0.0), axis=0, keepdims=True)
                fcar[s] = jnp.broadcast_to(tot, (8, LANES))
        else:
            fcar[...] = jnp.zeros((sb, 8, LANES), F32)

    x = x_ref[...]
    m = mod_ref[...]
    h = _rms(x, ng_ref[0:1, :]) * (1.0 + m[:, 1:2, :]) + m[:, 0:1, :]
    hb = h.reshape(rows, d).astype(BF16)
    qkv = _dot(hb, wqkv_ref[...])
    q = qkv[:, 0:d] * (FOX_HEAD_DIM ** -0.5 * LOG2E)
    k = qkv[:, d:2 * d]
    v = qkv[:, 2 * d:3 * d]
    q_ref[...] = q.astype(BF16).reshape(sb, tt, d)
    k_ref[...] = k.reshape(sb, tt, d)
    v_ref[...] = v.reshape(sb, tt, d)
    kb_ref[...] = k.astype(BF16).reshape(sb, tt, d)
    if past:
        vb_ref[...] = v.astype(BF16).reshape(sb, tt, d)
    else:
        for s in range(sb):
            vb_ref[s] = v[s * tt:(s + 1) * tt, :].T.astype(BF16)
    lf = jax.nn.log_sigmoid(_dot(hb, wf_ref[...]) + bf_ref[...])
    lf_ref[...] = lf[:, 0:nh].reshape(sb, tt, nh)
    for s in range(sb):
        outs, car = cumsum_rows(lambda r, n, s=s: lf[s * tt + r:s * tt + r + n, :], tt, fcar[s, 0:1, :])
        for bi, c in enumerate(outs):
            if past:
                fq_ref[s, bi * cblk:(bi + 1) * cblk, :] = c[:, 0:nh] * LOG2E
            else:
                qa, ka = _forget_lanes(c * LOG2E)
                qa_ref[s, bi * cblk:(bi + 1) * cblk, :] = qa
                ka_ref[s, bi * cblk:(bi + 1) * cblk, :] = ka
        fcar[s] = jnp.broadcast_to(car, (8, LANES))


def _fox_proj_call(x, mods, ng, lf_past, wqkv, wf, bf, sb, tt):
    bn, T, d = x.shape
    nh = FOX_HEADS
    past = lf_past is not None
    kern = functools.partial(_fox_proj_kernel, sb=sb, tt=tt, past=past)
    in_specs = [
        pl.BlockSpec((sb, tt, d), lambda b, t: (b, t, 0)),
        pl.BlockSpec((sb, 6, d), lambda b, t: (b, 0, 0)),
        _const_spec((4, d)),
    ]
    args = [x, mods, ng]
    if past:
        plen = lf_past.shape[2]
        in_specs.append(pl.BlockSpec((sb, nh, plen), lambda b, t: (b, 0, 0)))
        args.append(lf_past)
    in_specs += [_const_spec((d, 3 * d)), _const_spec((d, LANES)), _const_spec((1, LANES))]
    args += [wqkv, wf, jnp.pad(bf.reshape(1, nh), ((0, 0), (0, LANES - nh)))]
    tile = pl.BlockSpec((sb, tt, d), lambda b, t: (b, t, 0))
    small = pl.BlockSpec((sb, tt, nh), lambda b, t: (b, t, 0))
    out_specs = [tile, tile, tile, tile, tile, small, small]
    out_shape = [
        jax.ShapeDtypeStruct((bn, T, d), BF16),
        jax.ShapeDtypeStruct((bn, T, d), F32),
        jax.ShapeDtypeStruct((bn, T, d), F32),
        jax.ShapeDtypeStruct((bn, T, d), BF16),
        jax.ShapeDtypeStruct((bn, T, d), BF16),
        jax.ShapeDtypeStruct((bn, T, nh), F32),
        jax.ShapeDtypeStruct((bn, T, nh), F32),
    ]
    if past:
        out_specs.append(pl.BlockSpec((sb, nh, plen), lambda b, t: (b, 0, 0)))
        out_shape.append(jax.ShapeDtypeStruct((bn, nh, plen), F32))
    else:
        out_specs[4] = pl.BlockSpec((sb, d, tt), lambda b, t: (b, 0, t))
        out_shape[4] = jax.ShapeDtypeStruct((bn, d, T), BF16)
        wide = pl.BlockSpec((sb, tt, LANES), lambda b, t: (b, t, 0))
        out_specs[6:] = [wide, wide]
        out_shape[6:] = [jax.ShapeDtypeStruct((bn, T, LANES), BF16)] * 2
    return pl.pallas_call(
        kern,
        grid=(bn // sb, T // tt),
        in_specs=in_specs,
        out_specs=out_specs,
        out_shape=out_shape,
        scratch_shapes=[pltpu.VMEM((sb, 8, LANES), F32)],
        compiler_params=_params(("arbitrary", "arbitrary")),
        name="fox_proj",
    )(*args)


def _fox_attn_cached_kernel(x_ref, mod_ref, ng_ref, q_ref, kt_ref, vt_ref, fkt_ref, kn_ref, vn_ref, fq_ref,
                            fktn_ref, wo_ref, o_ref, acc, m_s, l_s, fq_s, *, tq):
    j = pl.program_id(1)
    nj = pl.num_programs(1)
    d = x_ref.shape[-1]
    npair = d // LANES
    hd = FOX_HEAD_DIM

    @pl.when(j == 0)
    def _():
        acc[...] = jnp.zeros(acc.shape, F32)
        l_s[...] = jnp.zeros(l_s.shape, F32)
        m_s[...] = jnp.full(m_s.shape, -1e30, F32)
        for jp in range(npair):
            fq_s[jp] = jnp.concatenate([jnp.broadcast_to(fq_ref[:, 2 * jp:2 * jp + 1], (tq, LANES)),
                                        jnp.broadcast_to(fq_ref[:, 2 * jp + 1:2 * jp + 2], (tq, LANES))], axis=0)

    low = lax.broadcasted_iota(jnp.int32, (tq, LANES), 1) < hd

    def attend(jp, s2, fk_a, fk_b, vmat, v_is_transposed, mask=None):
        nk = s2.shape[1]
        t = s2 - jnp.concatenate([jnp.broadcast_to(fk_a, (tq, nk)), jnp.broadcast_to(fk_b, (tq, nk))], axis=0)
        if mask is not None:
            t = jnp.where(mask, t, -jnp.inf)
        fq2 = fq_s[jp]
        m_old = m_s[jp]
        m_new = jnp.maximum(m_old, jnp.max(t, axis=-1, keepdims=True) + fq2)
        alpha = jnp.exp2(m_old - m_new)
        p = jnp.exp2(t - (m_new - fq2)[:, 0:1])
        l_s[jp] = alpha * l_s[jp] + jnp.sum(p, axis=-1, keepdims=True)
        m_s[jp] = m_new
        pb = p.astype(BF16)
        pv = _dot_nt(pb, vmat) if v_is_transposed else _dot(pb, vmat)
        acc[jp] = alpha * acc[jp] + pv

    def q_stack(jp):
        q2 = q_ref[:, jp * LANES:(jp + 1) * LANES]
        zq = jnp.zeros_like(q2)
        return jnp.concatenate([jnp.where(low, q2, zq), jnp.where(low, zq, q2)], axis=0)

    def past_logits(jp):
        return _dot(q_stack(jp), kt_ref[jp * LANES:(jp + 1) * LANES, :].astype(BF16))

    ahead = 2
    s2s = [past_logits(jp) for jp in range(ahead)]
    for jp in range(npair):
        s2 = s2s.pop(0)
        if jp + ahead < npair:
            s2s.append(past_logits(jp + ahead))
        vt2 = vt_ref[jp * LANES:(jp + 1) * LANES, :].astype(BF16)
        attend(jp, s2, fkt_ref[2 * jp:2 * jp + 1, :], fkt_ref[2 * jp + 1:2 * jp + 2, :], vt2, True)

    @pl.when(j == nj - 1)
    def _():
        rowq = lax.broadcasted_iota(jnp.int32, (2 * tq, LANES), 0)
        rowq = jnp.where(rowq >= tq, rowq - tq, rowq)
        causal = lax.broadcasted_iota(jnp.int32, (2 * tq, LANES), 1) <= rowq
        zpad = jnp.zeros((LANES - tq, LANES), BF16)

        def new_logits(jp):
            kn2 = jnp.concatenate([kn_ref[:, jp * LANES:(jp + 1) * LANES], zpad], axis=0)
            return _dot_nt(q_stack(jp), kn2)

        s2n = [new_logits(jp) for jp in range(npair)]
        for jp in range(npair):
            vn2 = jnp.concatenate([vn_ref[:, jp * LANES:(jp + 1) * LANES], zpad], axis=0)
            attend(jp, s2n[jp], fktn_ref[2 * jp:2 * jp + 1, :], fktn_ref[2 * jp + 1:2 * jp + 2, :],
                   vn2, False, causal)
        cols = []
        for jp in range(npair):
            a, l = acc[jp], l_s[jp]
            cols.append((jnp.where(low, a[0:tq], a[tq:2 * tq]) / jnp.where(low, l[0:tq], l[tq:2 * tq])).astype(BF16))
        y = _dot(jnp.concatenate(cols, axis=1), wo_ref[...])
        m = mod_ref[...]
        o_ref[...] = x_ref[...] + m[2:3, :] * _rms(y, ng_ref[1:2, :])


def _fox_attn_cached_call(x, mods, ng, q, kt_past, vt_past, fkt_past, kb_new, vb_new, fq, fkt_new, wo):
    bn, tq, d = x.shape
    plen = kt_past.shape[-1]
    nh = FOX_HEADS
    assert tq <= LANES and plen % 1024 == 0
    tkb = 1024
    kern = functools.partial(_fox_attn_cached_kernel, tq=tq)
    row = pl.BlockSpec((None, tq, d), lambda b, j: (b, 0, 0))
    return pl.pallas_call(
        kern,
        grid=(bn, plen // tkb),
        in_specs=[
            row,
            pl.BlockSpec((None, 6, d), lambda b, j: (b, 0, 0)),
            _const_spec((4, d)),
            row,
            pl.BlockSpec((None, d, tkb), lambda b, j: (b, 0, j)),
            pl.BlockSpec((None, d, tkb), lambda b, j: (b, 0, j)),
            pl.BlockSpec((None, nh, tkb), lambda b, j: (b, 0, j)),
            row,
            row,
            pl.BlockSpec((None, tq, nh), lambda b, j: (b, 0, 0)),
            pl.BlockSpec((None, nh, LANES), lambda b, j: (b, 0, 0)),
            _const_spec((d, d)),
        ],
        out_specs=row,
        out_shape=jax.ShapeDtypeStruct((bn, tq, d), F32),
        scratch_shapes=[
            pltpu.VMEM((d // LANES, 2 * tq, LANES), F32),
            pltpu.VMEM((d // LANES, 2 * tq, LANES), F32),
            pltpu.VMEM((d // LANES, 2 * tq, LANES), F32),
            pltpu.VMEM((d // LANES, 2 * tq, LANES), F32),
        ],
        compiler_params=_params(("arbitrary", "arbitrary")),
        name="fox_attn_cached",
    )(x, mods, ng, q, kt_past, vt_past, fkt_past, kb_new, vb_new, fq, fkt_new, wo)


def _fox_attn_t_kernel(qi_ref, ki_ref, x_ref, mod_ref, ng_ref, q_ref, k_ref, vt_ref, qa_ref, ka_ref, wo_ref,
                       o_ref, acct, m_s, *, tq, tk, rs):
    step_id = pl.program_id(1)
    qi = qi_ref[step_id]
    ki = ki_ref[step_id]
    d = x_ref.shape[-1]
    npair = d // LANES
    hd = FOX_HEAD_DIM
    nstrip = tk // rs
    nqb = tq // LANES

    @pl.when(ki == 0)
    def _():
        acct[...] = jnp.zeros(acct.shape, F32)
        m_s[...] = jnp.full(m_s.shape, -1e30, F32)

    lane_q = lax.broadcasted_iota(jnp.int32, (tq, LANES), 1)
    low = lane_q < hd
    ones_v = jnp.ones((hd, tk), BF16)

    def step(masked):
        if masked:
            dk = (lax.broadcasted_iota(jnp.int32, (rs, LANES), 0)
                  - lax.broadcasted_iota(jnp.int32, (rs, LANES), 1))
        qa = qa_ref[...]
        ka = ka_ref[...]

        def block_kind(i, c):
            if not masked:
                return 2
            if i * rs > c * LANES + LANES - 1:
                return 0
            if i * rs + rs - 1 <= c * LANES:
                return 2
            return 1

        def logits(st, i, c):
            t = st[i * rs:(i + 1) * rs, c * LANES:(c + 1) * LANES]
            if block_kind(i, c) == 1:
                t = jnp.where(dk <= (c * LANES - i * rs), t, -jnp.inf)
            return t

        def qk(hh):
            j, half = hh // 2, hh % 2
            q2 = q_ref[:, j * LANES:(j + 1) * LANES]
            qh = jnp.where(low if half == 0 else ~low, q2, jnp.zeros_like(q2))
            sel = ((lane_q & (FOX_HEADS - 1)) == hh) & (lane_q < 6 * FOX_HEADS)
            qcat = jnp.concatenate([qh, jnp.where(sel, qa, jnp.zeros_like(qa))], axis=1)
            kcat = jnp.concatenate([k_ref[:, j * LANES:(j + 1) * LANES], ka], axis=1)
            return _dot_nt(kcat, qcat)

        ahead = 2
        sts = [qk(h0) for h0 in range(ahead)]
        for hh in range(2 * npair):
            j, half = hh // 2, hh % 2
            st = sts.pop(0)
            if hh + ahead < 2 * npair:
                sts.append(qk(hh + ahead))
            vt2 = vt_ref[j * LANES:(j + 1) * LANES, :]
            m_old = m_s[hh:hh + 1, :]
            pm = [None] * nqb
            for i in range(nstrip):
                for c in range(nqb):
                    if block_kind(i, c) > 0:
                        t = logits(st, i, c)
                        pm[c] = t if pm[c] is None else jnp.maximum(pm[c], t)
            mx = jnp.concatenate([jnp.max(p_, axis=0, keepdims=True) for p_ in pm], axis=1)
            m_new = jnp.maximum(m_old, mx)
            alpha = jnp.exp2(m_old - m_new)
            prow = []
            for i in range(nstrip):
                pieces = []
                for c in range(nqb):
                    if block_kind(i, c) > 0:
                        pieces.append(jnp.exp2(logits(st, i, c)
                                               - m_new[:, c * LANES:(c + 1) * LANES]).astype(BF16))
                    else:
                        pieces.append(jnp.zeros((rs, LANES), BF16))
                prow.append(jnp.concatenate(pieces, axis=1))
            vaug = jnp.concatenate([vt2[half * hd:(half + 1) * hd, :], ones_v], axis=0)
            ot = _dot(vaug, jnp.concatenate(prow, axis=0))
            acct[hh] = alpha * acct[hh] + ot
            m_s[hh:hh + 1, :] = m_new

    @pl.when(ki == qi)
    def _():
        step(True)

    @pl.when(ki < qi)
    def _():
        step(False)

    @pl.when(ki == qi)
    def _():
        ots = []
        for hh in range(2 * npair):
            a = acct[hh]
            ots.append(a[0:hd] / a[hd:hd + 1])
        o = jnp.concatenate(ots, axis=0).T.astype(BF16)
        y = _dot(o, wo_ref[...])
        m = mod_ref[...]
        o_ref[...] = x_ref[...] + m[2:3, :] * _rms(y, ng_ref[1:2, :])


def _fox_attn_t_call(x, mods, ng, q, kb, vt, qa, ka, wo, tq):
    bn, T, d = x.shape
    nh = FOX_HEADS
    tk = tq
    kern = functools.partial(_fox_attn_t_kernel, tq=tq, tk=tk, rs=32)
    pairs = [(i, j) for i in range(T // tq) for j in range(i + 1)]
    qi_tab = jnp.asarray([p_[0] for p_ in pairs], jnp.int32)
    ki_tab = jnp.asarray([p_[1] for p_ in pairs], jnp.int32)
    qmap = lambda b, n, qt, kt: (b, qt[n], 0)
    kmap = lambda b, n, qt, kt: (b, kt[n], 0)
    grid_spec = pltpu.PrefetchScalarGridSpec(
        num_scalar_prefetch=2,
        grid=(bn, len(pairs)),
        in_specs=[
            pl.BlockSpec((None, tq, d), qmap),
            pl.BlockSpec((None, 6, d), lambda b, n, qt, kt: (b, 0, 0)),
            _const_spec((4, d)),
            pl.BlockSpec((None, tq, d), qmap),
            pl.BlockSpec((None, tk, d), kmap),
            pl.BlockSpec((None, d, tk), lambda b, n, qt, kt: (b, 0, kt[n])),
            pl.BlockSpec((None, tq, LANES), qmap),
            pl.BlockSpec((None, tk, LANES), kmap),
            _const_spec((d, d)),
        ],
        out_specs=pl.BlockSpec((None, tq, d), qmap),
        scratch_shapes=[
            pltpu.VMEM((nh, LANES, tq), F32),
            pltpu.VMEM((nh, tq), F32),
        ],
    )
    return pl.pallas_call(
        kern,
        grid_spec=grid_spec,
        out_shape=jax.ShapeDtypeStruct((bn, T, d), F32),
        compiler_params=_params(("arbitrary", "arbitrary")),
        name="fox_attn_t",
    )(qi_tab, ki_tab, x, mods, ng, q, kb, vt, qa, ka, wo)


def _prep_weights(W):
    bf = lambda a: a.astype(BF16)
    di = W["ssd_w_out"].shape[0]
    nh = di // SSD_HEAD_DIM
    cdim = W["ssd_conv_w"].shape[-1]
    d = W["ssd_w_in"].shape[0]
    lru = []
    for j in range(W["lru_w_x"].shape[0]):
        lru.append(dict(
            wx=bf(W["lru_w_x"][j]), bx=W["lru_b_x"][j], wy=bf(W["lru_w_y"][j]), by=W["lru_b_y"][j],
            cw=W["lru_conv_w"][j], cb=W["lru_conv_b"][j],
            wai=bf(0.5 * jnp.concatenate([W["lru_w_a"][j], W["lru_w_i"][j]], axis=-1)),
            ba=0.5 * W["lru_b_a"][j], bi=0.5 * W["lru_b_i"][j], lam=W["lru_lambda"][j],
            wo=bf(W["lru_w_o"][j]), bo=W["lru_b_o"][j]))
    w_in = W["ssd_w_in"]
    pad_l = lambda v: jnp.pad(v.reshape(1, -1), ((0, 0), (0, LANES - v.shape[-1])))
    onehot = (jnp.arange(LANES)[:, None] == jnp.arange(di)[None, :] // SSD_HEAD_DIM).astype(BF16)
    ssd = dict(
        e2=jnp.concatenate([onehot, onehot], axis=0),
        wz=bf(w_in[:, :di]), wxbc=bf(w_in[:, di:di + cdim]),
        wdt=bf(jnp.pad(w_in[:, di + cdim:], ((0, 0), (0, LANES - nh)))),
        cw=W["ssd_conv_w"], cb=W["ssd_conv_b"].reshape(1, cdim),
        dtb=pad_l(W["ssd_dt_bias"]), alog=pad_l(W["ssd_a_log"]),
        dsk=jnp.repeat(W["ssd_d"], SSD_HEAD_DIM).reshape(1, di), sng=W["ssd_norm_g"].reshape(1, di),
        wout=bf(W["ssd_w_out"]))
    fox = dict(
        wqkv=bf(W["fox_w_qkv"]),
        wf=bf(jnp.pad(W["fox_w_f"], ((0, 0), (0, LANES - W["fox_w_f"].shape[-1])))),
        bf=W["fox_b_f"], wo=bf(W["fox_w_o"]))
    ffn = dict(wg=bf(W["ffn_w_gate"]), wu=bf(W["ffn_w_up"]), wd=bf(W["ffn_w_down"]),
               cw=W["ffn_conv_w"], cb=W["ffn_conv_b"])
    return dict(lru=lru, ssd=ssd, fox=fox, ffn=ffn, norm_g=W["norm_g"])


def _run_trunk(x, mods, prev, P, cfg):
    bn, T, d = x.shape
    sb, tt, tt_ffn, tq = cfg["sb"], cfg["tt"], cfg["tt_ffn"], cfg["tq"]
    depth = mods.shape[0]
    new = {}
    ffn_bufs = []
    for layer in range(depth):
        ng = P["norm_g"][layer]
        ml = mods[layer]
        kind = layer % 3
        tag = "l%d" % layer
        if kind == 0:
            x, cn, hn = _lru_call(x, ml, ng, prev[tag + "_conv"], prev[tag + "_h"], P["lru"][layer // 3], sb,
                                  cfg["tt_big"])
            new[tag + "_conv"], new[tag + "_h"] = cn, hn
        elif kind == 1:
            x, cn, sn = _ssd_call(x, ml, ng, prev[tag + "_conv"], prev[tag + "_ssm"], P["ssd"], cfg["sb_ssd"], tt)
            new[tag + "_conv"], new[tag + "_ssm"] = cn, sn
        else:
            fx = P["fox"]
            kp, vp, lp = prev.get(tag + "_k"), prev.get(tag + "_v"), prev.get(tag + "_logf")
            lpt = None if lp is None else jnp.swapaxes(lp, 1, 2)
            res = _fox_proj_call(x, ml, ng, lpt, fx["wqkv"], fx["wf"], fx["bf"], sb, cfg["tt_big"])
            q, k, v, kb, vb, lf, fq = res[:7]
            if kp is None:
                x = _fox_attn_t_call(x, ml, ng, q, kb, vb, res[6], res[7], fx["wo"], tq)
            else:
                plen = kp.shape[1]
                kt_past = jnp.transpose(kp, (0, 2, 3, 1)).reshape(bn, d, plen)
                vt_past = jnp.transpose(vp, (0, 2, 3, 1)).reshape(bn, d, plen)
                fkt_new = jnp.pad(jnp.swapaxes(fq, 1, 2), ((0, 0), (0, 0), (0, LANES - T)))
                x = _fox_attn_cached_call(x, ml, ng, q, kt_past, vt_past, res[7], kb, vb, fq, fkt_new, fx["wo"])
            new[tag + "_k"] = k.reshape(bn, T, FOX_HEADS, FOX_HEAD_DIM)
            new[tag + "_v"] = v.reshape(bn, T, FOX_HEADS, FOX_HEAD_DIM)
            new[tag + "_logf"] = lf
        f = P["ffn"]
        x, buf = _ffn_call(x, ml, ng, prev["ffn_conv"][layer], f["wg"], f["wu"], f["cw"][layer], f["cb"][layer],
                           f["wd"], layer, sb, tt_ffn)
        ffn_bufs.append(buf)
    new["ffn_conv"] = jnp.stack(ffn_bufs)
    return x, new


def _fresh_state(bn, W):
    dr = W["lru_w_x"].shape[-1]
    cdim = W["ssd_conv_w"].shape[-1]
    di = W["ssd_w_out"].shape[0]
    f = W["ffn_w_gate"].shape[-1]
    depth = W["ffn_w_gate"].shape[0]
    st = dict(ffn_conv=jnp.zeros((depth, bn, FFN_CONV - 1, f), F32))
    for layer in range(depth):
        tag = "l%d" % layer
        if layer % 3 == 0:
            st[tag + "_conv"] = jnp.zeros((bn, LRU_CONV - 1, dr), F32)
            st[tag + "_h"] = jnp.zeros((bn, dr), F32)
        elif layer % 3 == 1:
            st[tag + "_conv"] = jnp.zeros((bn, SSD_CONV - 1, cdim), F32)
            st[tag + "_ssm"] = jnp.zeros((bn, di // SSD_HEAD_DIM, SSD_HEAD_DIM, SSD_STATE), F32)
    return st


def _tile_cfg(bn, T):
    if T >= 256:
        big = 512 if T % 512 == 0 else 256
        return dict(sb=1, sb_ssd=1, tt=256, tt_big=big, tt_ffn=256, tq=big)
    sb = 4 if bn % 4 == 0 else 1
    return dict(sb=sb, sb_ssd=2 if bn % 2 == 0 else 1, tt=T, tt_big=T, tt_ffn=T, tq=T)


def kernel(x_prompt, x_sample, c_prompt, c_sample, state_l0_conv, state_l0_h, state_l1_conv, state_l1_ssm, cache_l2_k, cache_l2_v, cache_l2_logf, state_l3_conv, state_l3_h, state_ffn_conv, ada_w, ada_b, norm_g, lru_w_x, lru_b_x, lru_w_y, lru_b_y, lru_conv_w, lru_conv_b, lru_w_a, lru_b_a, lru_w_i, lru_b_i, lru_lambda, lru_w_o, lru_b_o, ssd_w_in, ssd_conv_w, ssd_conv_b, ssd_dt_bias, ssd_a_log, ssd_d, ssd_norm_g, ssd_w_out, fox_w_qkv, fox_w_f, fox_b_f, fox_w_o, ffn_w_gate, ffn_w_up, ffn_conv_w, ffn_conv_b, ffn_w_down):
    W = dict(ada_w=ada_w, ada_b=ada_b, norm_g=norm_g,
             lru_w_x=lru_w_x, lru_b_x=lru_b_x, lru_w_y=lru_w_y, lru_b_y=lru_b_y,
             lru_conv_w=lru_conv_w, lru_conv_b=lru_conv_b, lru_w_a=lru_w_a, lru_b_a=lru_b_a,
             lru_w_i=lru_w_i, lru_b_i=lru_b_i, lru_lambda=lru_lambda, lru_w_o=lru_w_o, lru_b_o=lru_b_o,
             ssd_w_in=ssd_w_in, ssd_conv_w=ssd_conv_w, ssd_conv_b=ssd_conv_b, ssd_dt_bias=ssd_dt_bias,
             ssd_a_log=ssd_a_log, ssd_d=ssd_d, ssd_norm_g=ssd_norm_g, ssd_w_out=ssd_w_out,
             fox_w_qkv=fox_w_qkv, fox_w_f=fox_w_f, fox_b_f=fox_b_f, fox_w_o=fox_w_o,
             ffn_w_gate=ffn_w_gate, ffn_w_up=ffn_w_up, ffn_conv_w=ffn_conv_w, ffn_conv_b=ffn_conv_b,
             ffn_w_down=ffn_w_down)
    P = _prep_weights(W)
    bp, bs = x_prompt.shape[0], x_sample.shape[0]
    d = x_prompt.shape[-1]
    depth = ada_w.shape[0]
    mods = _ada_call(jnp.concatenate([c_prompt, c_sample], axis=0), ada_w, ada_b)
    mods = mods.reshape(depth, bp + bs, 6, d)

    y_prompt, p = _run_trunk(x_prompt, mods[:, :bp], _fresh_state(bp, W), P, _tile_cfg(bp, x_prompt.shape[1]))
    prev = dict(l0_conv=state_l0_conv, l0_h=state_l0_h, l1_conv=state_l1_conv, l1_ssm=state_l1_ssm,
                l2_k=cache_l2_k, l2_v=cache_l2_v, l2_logf=cache_l2_logf,
                l3_conv=state_l3_conv, l3_h=state_l3_h, ffn_conv=state_ffn_conv)
    y_sample, s = _run_trunk(x_sample, mods[:, bp:], prev, P, _tile_cfg(bs, x_sample.shape[1]))
    return (y_prompt, y_sample,
            p['l0_conv'], p['l0_h'], p['l1_conv'], p['l1_ssm'], p['l2_k'], p['l2_v'], p['l2_logf'],
            p['l3_conv'], p['l3_h'], p['ffn_conv'],
            s['l0_conv'], s['l0_h'], s['l1_conv'], s['l1_ssm'], s['l2_k'], s['l2_v'], s['l2_logf'],
            s['l3_conv'], s['l3_h'], s['ffn_conv'])
```
